```python
import math
import jax, jax.numpy as jnp
from jax import lax
import numpy as np

D_MODEL = 1024
BATCH = 1
SEQ = 16384
DEPTH = 4

GRID_W = 64
CTX_LEN = 256
N_MIXERS = 3
EPS = 1e-6

SSD_EXPAND = 2
SSD_INNER = SSD_EXPAND * D_MODEL
SSD_HEAD_DIM = 64
SSD_HEADS = SSD_INNER // SSD_HEAD_DIM
SSD_GROUPS = 8
SSD_HPG = SSD_HEADS // SSD_GROUPS
SSD_STATE = 128
SSD_GN = SSD_GROUPS * SSD_STATE
SSD_CONV = 5
SSD_CHUNK = 128
SSD_IN = 2 * SSD_INNER + 2 * SSD_GN + 2 * SSD_HEADS

DIF_HEAD_DIM = 64
DIF_HEADS = D_MODEL // (2 * DIF_HEAD_DIM)
DIF_V_DIM = 2 * DIF_HEAD_DIM
DIF_QK = DIF_HEADS * 2 * DIF_HEAD_DIM
DIF_VW = DIF_HEADS * DIF_V_DIM
DIF_IN = 2 * DIF_QK + 2 * DIF_VW
Q_BLOCK = 128
ROPE_BASE = 10000.0

NA_HEAD_DIM = 64
NA_HEADS = D_MODEL // NA_HEAD_DIM
NA_KR = 8
NA_KC = 16
NA_IN = 4 * D_MODEL

kernel_name = "hybrid_ssd_diffattn_natten_prefix_dit"


def _rmsnorm(x, w):
    xf = x.astype(jnp.float32)
    y = xf * lax.rsqrt(jnp.mean(xf * xf, axis=-1, keepdims=True) + EPS)
    return (y * w.astype(jnp.float32)).astype(x.dtype)


def _flip(t):
    return jnp.flip(t, axis=1)


def _dwconv(u, w, b):
    ch = u.shape[-1]
    y = lax.conv_general_dilated(
        u, w[:, None, :].astype(u.dtype), window_strides=(1,),
        padding=[(SSD_CONV // 2, SSD_CONV // 2)],
        dimension_numbers=('NWC', 'WIO', 'NWC'), feature_group_count=ch)
    return y + b.astype(u.dtype)


def _segsum(a):
    t = a.shape[-1]
    cs = jnp.cumsum(a, axis=-1)
    diff = cs[..., :, None] - cs[..., None, :]
    mask = jnp.tril(jnp.ones((t, t), dtype=bool))
    return jnp.where(mask, diff, -jnp.inf)


def _ssd_scan(x, a, b, c, h0):
    bsz, n, g, r, p = x.shape
    nst = b.shape[-1]
    nc = n // SSD_CHUNK
    x = x.reshape(bsz, nc, SSD_CHUNK, g, r, p)
    a = a.reshape(bsz, nc, SSD_CHUNK, g, r).transpose(0, 3, 4, 1, 2)
    b = b.reshape(bsz, nc, SSD_CHUNK, g, nst)
    c = c.reshape(bsz, nc, SSD_CHUNK, g, nst)
    a_cs = jnp.cumsum(a, axis=-1)
    decay = jnp.exp(_segsum(a))
    cb = jnp.einsum('bclgn,bcsgn->bcgls', c, b)
    y_diag = jnp.einsum('bcgls,bgrcls,bcsgrp->bclgrp', cb, decay, x)
    decay_states = jnp.exp(a_cs[..., -1:] - a_cs)
    states = jnp.einsum('bcsgn,bgrcs,bcsgrp->bcgrpn', b, decay_states, x)
    states = jnp.concatenate([h0[:, None], states], axis=1)
    chunk_decay = jnp.exp(_segsum(jnp.pad(a_cs[..., -1], ((0, 0), (0, 0), (0, 0), (1, 0)))))
    states = jnp.einsum('bgrzc,bcgrpn->bzgrpn', chunk_decay, states)
    prev_states, final = states[:, :-1], states[:, -1]
    y_off = jnp.einsum('bclgn,bcgrpn,bgrcl->bclgrp', c, prev_states, jnp.exp(a_cs))
    return (y_diag + y_off).reshape(bsz, n, g, r, p), final


def _ssd_mixer(hc, hl, w_in, conv_w, conv_b, dt_bias, a_log, d_skip, norm_w, w_out, ctx_out):
    f32 = jnp.float32

    def project(h):
        bsz, n = h.shape[:2]
        u = h @ w_in
        z = u[..., :SSD_INNER]
        xbc = jax.nn.silu(_dwconv(u[..., SSD_INNER:2 * SSD_INNER + 2 * SSD_GN], conv_w, conv_b)).astype(f32)
        dt = u[..., 2 * SSD_INNER + 2 * SSD_GN:].astype(f32).reshape(bsz, n, 2, SSD_GROUPS, SSD_HPG)
        xs = xbc[..., :SSD_INNER].reshape(bsz, n, SSD_GROUPS, SSD_HPG, SSD_HEAD_DIM)
        bm = xbc[..., SSD_INNER:SSD_INNER + SSD_GN].reshape(bsz, n, SSD_GROUPS, SSD_STATE)
        cm = xbc[..., SSD_INNER + SSD_GN:].reshape(bsz, n, SSD_GROUPS, SSD_STATE)
        return z, xs, bm, cm, dt

    zc, xs_c, b_c, c_c, dt_c = project(hc)
    zl, xs_l, b_l, c_l, dt_l = project(hl)
    bsz = hl.shape[0]
    d_h = d_skip.astype(f32).reshape(SSD_GROUPS, SSD_HPG, 1)
    y_l = xs_l * d_h
    y_c = xs_c * d_h if ctx_out else None
    for direction in range(2):
        order = _flip if direction == 1 else (lambda t: t)
        a_coef = -jnp.exp(a_log[direction].astype(f32)).reshape(SSD_GROUPS, SSD_HPG)
        bias = dt_bias[direction].astype(f32).reshape(SSD_GROUPS, SSD_HPG)
        dtd_c = jax.nn.softplus(dt_c[:, :, direction] + bias)
        dtd_l = jax.nn.softplus(dt_l[:, :, direction] + bias)
        h0 = jnp.zeros((bsz, SSD_GROUPS, SSD_HPG, SSD_HEAD_DIM, SSD_STATE), f32)
        yc_d, h_ctx = _ssd_scan(order(xs_c * dtd_c[..., None]), order(dtd_c * a_coef),
                                order(b_c), order(c_c), h0)
        yl_d, _ = _ssd_scan(order(xs_l * dtd_l[..., None]), order(dtd_l * a_coef),
                            order(b_l), order(c_l), h_ctx)
        y_l = y_l + order(yl_d)
        if ctx_out:
            y_c = y_c + order(yc_d)

    def out(y, z):
        bsz_, n = z.shape[:2]
        y = y.reshape(bsz_, n, SSD_INNER).astype(z.dtype)
        return _rmsnorm(y * jax.nn.silu(z), norm_w) @ w_out

    ol = out(y_l, zl)
    oc = out(y_c, zc) if ctx_out else None
    return oc, ol


def _axial_rope(n_tok, dim):
    t = jnp.arange(n_tok)
    row = (t // GRID_W).astype(jnp.float32)
    col = (t % GRID_W).astype(jnp.float32)
    n_freq = dim // 4
    inv = ROPE_BASE ** (-jnp.arange(n_freq, dtype=jnp.float32) / n_freq)
    ang = jnp.concatenate([row[:, None] * inv, col[:, None] * inv], axis=-1)
    return jnp.cos(ang), jnp.sin(ang)


def _apply_rope(x, cos, sin):
    half = x.shape[-1] // 2
    shape = (1, x.shape[1]) + (1,) * (x.ndim - 3) + (half,)
    cos = cos.reshape(shape).astype(x.dtype)
    sin = sin.reshape(shape).astype(x.dtype)
    x1, x2 = x[..., :half], x[..., half:]
    return jnp.concatenate([x1 * cos - x2 * sin, x1 * sin + x2 * cos], axis=-1)


def _diff_attend(q, k, v, lam):
    s = jnp.einsum('bqhid,bkhid->bhiqk', q, k).astype(jnp.float32) * (DIF_HEAD_DIM ** -0.5)
    p = jax.nn.softmax(s, axis=-1)
    w = p[:, :, 0] - lam * p[:, :, 1]
    return jnp.einsum('bhqk,bkhe->bqhe', w.astype(v.dtype), v)


def _diff_attn_mixer(hc, hl, w_in, lam_q1, lam_k1, lam_q2, lam_k2, subln_w, w_out,
                     lam_init, cos, sin, ctx_out):
    def project(h):
        bsz, n = h.shape[:2]
        u = h @ w_in
        q = u[..., :DIF_QK].reshape(bsz, n, DIF_HEADS, 2, DIF_HEAD_DIM)
        k = u[..., DIF_QK:2 * DIF_QK].reshape(bsz, n, DIF_HEADS, 2, DIF_HEAD_DIM)
        v = u[..., 2 * DIF_QK:2 * DIF_QK + DIF_VW].reshape(bsz, n, DIF_HEADS, DIF_V_DIM)
        g = u[..., 2 * DIF_QK + DIF_VW:]
        return q, k, v, g

    f32 = jnp.float32
    lam = (jnp.exp(jnp.sum(lam_q1.astype(f32) * lam_k1.astype(f32)))
           - jnp.exp(jnp.sum(lam_q2.astype(f32) * lam_k2.astype(f32))) + lam_init)
    qc, kc, vc, gc = project(hc)
    ql, kl, vl, gl = project(hl)
    ql = _apply_rope(ql, cos, sin)
    kl = _apply_rope(kl, cos, sin)
    k_all = jnp.concatenate([kc, kl], axis=1)
    v_all = jnp.concatenate([vc, vl], axis=1)
    bsz, n = hl.shape[:2]
    nb = n // Q_BLOCK
    q_blocks = ql.reshape(bsz, nb, Q_BLOCK, DIF_HEADS, 2, DIF_HEAD_DIM).transpose(1, 0, 2, 3, 4, 5)
    o_l = lax.map(lambda qb: _diff_attend(qb, k_all, v_all, lam), q_blocks)
    o_l = o_l.transpose(1, 0, 2, 3, 4).reshape(bsz, n, DIF_HEADS, DIF_V_DIM)

    def out(o, g):
        o = _rmsnorm(o, subln_w) * (1.0 - lam_init)
        return (o.reshape(g.shape) * jax.nn.silu(g)) @ w_out

    ol = out(o_l, gl)
    oc = out(_diff_attend(qc, kc, vc, lam), gc) if ctx_out else None
    return oc, ol


def _softmax_attend(q, k, v, scale):
    s = jnp.einsum('bqhd,bkhd->bhqk', q, k).astype(jnp.float32) * scale
    p = jax.nn.softmax(s, axis=-1).astype(v.dtype)
    return jnp.einsum('bhqk,bkhd->bqhd', p, v)


def _na_mixer(hc, hl, w_in, rpb, w_out, ctx_out):
    def project(h):
        bsz, n = h.shape[:2]
        u = h @ w_in
        q = u[..., :D_MODEL].reshape(bsz, n, NA_HEADS, NA_HEAD_DIM)
        k = u[..., D_MODEL:2 * D_MODEL].reshape(bsz, n, NA_HEADS, NA_HEAD_DIM)
        v = u[..., 2 * D_MODEL:3 * D_MODEL].reshape(bsz, n, NA_HEADS, NA_HEAD_DIM)
        g = u[..., 3 * D_MODEL:]
        return q, k, v, g

    scale = NA_HEAD_DIM ** -0.5
    qc, kc, vc, gc = project(hc)
    ql, kl, vl, gl = project(hl)
    bsz, n = hl.shape[:2]
    rows = n // GRID_W
    kr = min(NA_KR, rows)
    kw = min(NA_KC, GRID_W)
    col_start = np.clip(np.arange(GRID_W) - kw // 2, 0, GRID_W - kw)
    col_idx = col_start[:, None] + np.arange(kw)[None, :]
    col_bias_idx = col_idx - np.arange(GRID_W)[:, None] + (NA_KC - 1)
    k_grid = kl.reshape(bsz, rows, GRID_W, NA_HEADS, NA_HEAD_DIM)
    v_grid = vl.reshape(bsz, rows, GRID_W, NA_HEADS, NA_HEAD_DIM)
    n_loc = kr * kw

    def row_block(args):
        r, q_row = args
        r0 = jnp.clip(r - kr // 2, 0, rows - kr)
        k_band = lax.dynamic_slice_in_dim(k_grid, r0, kr, axis=1)
        v_band = lax.dynamic_slice_in_dim(v_grid, r0, kr, axis=1)
        k_win = k_band[:, :, col_idx]
        v_win = v_band[:, :, col_idx]
        bias = rpb[:, r0 + jnp.arange(kr) - r + (NA_KR - 1)][:, :, col_bias_idx]
        s_loc = (jnp.einsum('bwhd,brwkhd->bhwrk', q_row, k_win).astype(jnp.float32) * scale
                 + bias.transpose(0, 2, 1, 3).astype(jnp.float32)[None])
        s_ctx = jnp.einsum('bwhd,bkhd->bhwk', q_row, kc).astype(jnp.float32) * scale
        s = jnp.concatenate([s_loc.reshape(bsz, NA_HEADS, GRID_W, n_loc), s_ctx], axis=-1)
        p = jax.nn.softmax(s, axis=-1).astype(v_grid.dtype)
        p_loc = p[..., :n_loc].reshape(bsz, NA_HEADS, GRID_W, kr, kw)
        p_ctx = p[..., n_loc:]
        return (jnp.einsum('bhwrk,brwkhd->bwhd', p_loc, v_win)
                + jnp.einsum('bhwk,bkhd->bwhd', p_ctx, vc))

    q_rows = ql.reshape(bsz, rows, GRID_W, NA_HEADS, NA_HEAD_DIM).transpose(1, 0, 2, 3, 4)
    o_l = lax.map(row_block, (jnp.arange(rows), q_rows))
    o_l = o_l.transpose(1, 0, 2, 3, 4).reshape(bsz, n, D_MODEL)
    ol = (o_l * jax.nn.silu(gl)) @ w_out
    oc = None
    if ctx_out:
        o_c = _softmax_attend(qc, kc, vc, scale).reshape(gc.shape)
        oc = (o_c * jax.nn.silu(gc)) @ w_out
    return oc, ol


def setup_inputs(seed: int = 0) -> dict:
    key = jax.random.key(seed)
    ks = iter(jax.random.split(key, 32))
    f32 = jnp.float32

    def nrm(shape, s):
        return s * jax.random.normal(next(ks), shape, f32)

    n_a = len(range(0, DEPTH, N_MIXERS))
    n_b = len(range(1, DEPTH, N_MIXERS))
    n_c = len(range(2, DEPTH, N_MIXERS))
    x = nrm((BATCH, SEQ, D_MODEL), 1.0)
    c = nrm((BATCH, D_MODEL), 1.0)
    ctx = nrm((BATCH, CTX_LEN, D_MODEL), 1.0)
    c_ctx = nrm((D_MODEL,), 1.0)
    ada_w = nrm((DEPTH, D_MODEL, 3 * D_MODEL), D_MODEL ** -0.5)
    ada_b = nrm((DEPTH, 3 * D_MODEL), 0.02)
    norm_pre = 1.0 + nrm((DEPTH, D_MODEL), 0.02)
    norm_post = 1.0 + nrm((DEPTH, D_MODEL), 0.02)
    ssd_w_in = nrm((n_a, D_MODEL, SSD_IN), D_MODEL ** -0.5)
    ssd_conv_w = nrm((n_a, SSD_CONV, SSD_INNER + 2 * SSD_GN), SSD_CONV ** -0.5)
    ssd_conv_b = nrm((n_a, SSD_INNER + 2 * SSD_GN), 0.02)
    dt0 = jnp.exp(jax.random.uniform(next(ks), (n_a, 2, SSD_HEADS), f32,
                                     minval=math.log(1e-3), maxval=math.log(1e-1)))
    ssd_dt_bias = dt0 + jnp.log(-jnp.expm1(-dt0))
    ssd_a_log = jnp.log(jax.random.uniform(next(ks), (n_a, 2, SSD_HEADS), f32, minval=1.0, maxval=16.0))
    ssd_d = 1.0 + nrm((n_a, SSD_HEADS), 0.1)
    ssd_norm = 1.0 + nrm((n_a, SSD_INNER), 0.02)
    ssd_w_out = nrm((n_a, SSD_INNER, D_MODEL), SSD_INNER ** -0.5)
    dif_w_in = nrm((n_b, D_MODEL, DIF_IN), D_MODEL ** -0.5)
    dif_lam_q1 = nrm((n_b, DIF_HEAD_DIM), 0.1)
    dif_lam_k1 = nrm((n_b, DIF_HEAD_DIM), 0.1)
    dif_lam_q2 = nrm((n_b, DIF_HEAD_DIM), 0.1)
    dif_lam_k2 = nrm((n_b, DIF_HEAD_DIM), 0.1)
    dif_subln = 1.0 + nrm((n_b, DIF_V_DIM), 0.02)
    dif_w_out = nrm((n_b, DIF_VW, D_MODEL), DIF_VW ** -0.5)
    na_w_in = nrm((n_c, D_MODEL, NA_IN), D_MODEL ** -0.5)
    na_rpb = nrm((n_c, NA_HEADS, 2 * NA_KR - 1, 2 * NA_KC - 1), 0.1)
    na_w_out = nrm((n_c, D_MODEL, D_MODEL), D_MODEL ** -0.5)
    return {"x": x, "c": c, "ctx": ctx, "c_ctx": c_ctx,
            "ada_w": ada_w, "ada_b": ada_b, "norm_pre": norm_pre, "norm_post": norm_post,
            "ssd_w_in": ssd_w_in, "ssd_conv_w": ssd_conv_w, "ssd_conv_b": ssd_conv_b,
            "ssd_dt_bias": ssd_dt_bias, "ssd_a_log": ssd_a_log, "ssd_d": ssd_d,
            "ssd_norm": ssd_norm, "ssd_w_out": ssd_w_out,
            "dif_w_in": dif_w_in, "dif_lam_q1": dif_lam_q1, "dif_lam_k1": dif_lam_k1,
            "dif_lam_q2": dif_lam_q2, "dif_lam_k2": dif_lam_k2, "dif_subln": dif_subln,
            "dif_w_out": dif_w_out,
            "na_w_in": na_w_in, "na_rpb": na_rpb, "na_w_out": na_w_out}


def reference(x, c, ctx, c_ctx, ada_w, ada_b, norm_pre, norm_post,
              ssd_w_in, ssd_conv_w, ssd_conv_b, ssd_dt_bias, ssd_a_log, ssd_d, ssd_norm, ssd_w_out,
              dif_w_in, dif_lam_q1, dif_lam_k1, dif_lam_q2, dif_lam_k2, dif_subln, dif_w_out,
              na_w_in, na_rpb, na_w_out):
    n_lat = x.shape[1]
    cos, sin = _axial_rope(n_lat, DIF_HEAD_DIM)
    s_lat = jax.nn.silu(c)
    s_ctx = jax.nn.silu(c_ctx)
    xl, xc = x, ctx
    ia = ib = ic = 0
    for i in range(DEPTH):
        ctx_out = i < DEPTH - 1
        mod_l = (s_lat @ ada_w[i] + ada_b[i])[:, None, :]
        mod_c = s_ctx @ ada_w[i] + ada_b[i]
        shift_l, scale_l, gate_l = jnp.split(mod_l, 3, axis=-1)
        shift_c, scale_c, gate_c = jnp.split(mod_c, 3, axis=-1)
        hl = _rmsnorm(xl, norm_pre[i]) * (1.0 + scale_l) + shift_l
        hc = _rmsnorm(xc, norm_pre[i]) * (1.0 + scale_c) + shift_c
        kind = i % N_MIXERS
        if kind == 0:
            oc, ol = _ssd_mixer(hc, hl, ssd_w_in[ia], ssd_conv_w[ia], ssd_conv_b[ia], ssd_dt_bias[ia],
                                ssd_a_log[ia], ssd_d[ia], ssd_norm[ia], ssd_w_out[ia], ctx_out)
            ia += 1
        elif kind == 1:
            lam_init = 0.8 - 0.6 * math.exp(-0.3 * i)
            oc, ol = _diff_attn_mixer(hc, hl, dif_w_in[ib], dif_lam_q1[ib], dif_lam_k1[ib],
                                      dif_lam_q2[ib], dif_lam_k2[ib], dif_subln[ib], dif_w_out[ib],
                                      lam_init, cos, sin, ctx_out)
            ib += 1
        else:
            oc, ol = _na_mixer(hc, hl, na_w_in[ic], na_rpb[ic], na_w_out[ic], ctx_out)
            ic += 1
        xl = xl + gate_l * _rmsnorm(ol, norm_post[i])
        if ctx_out:
            xc = xc + gate_c * _rmsnorm(oc, norm_post[i])
    return xl
```

```python
import functools
import math

import numpy as np
import jax
import jax.numpy as jnp
from jax import lax
from jax.experimental import pallas as pl
from jax.experimental.pallas import tpu as pltpu

F32 = jnp.float32
MM = jnp.bfloat16

EPS = 1e-6
GRID_W = 64
LANES = 128
HEAD = 64
NEG = -1e30

SSD_HEADS = 32
SSD_STATE = 128
SSD_CONV = 5
SSD_CHUNK = 128

NA_KR = 8
NA_KC = 16
ROPE_BASE = 10000.0

VMEM_LIMIT = 52 * 1024 * 1024


def _cparams(*sem):
    return pltpu.CompilerParams(dimension_semantics=sem, vmem_limit_bytes=VMEM_LIMIT)


def _silu(v):
    return v * (1.0 / (1.0 + jnp.exp(-v)))


def _softplus(v):
    return jnp.maximum(v, 0.0) + jnp.log(1.0 + jnp.exp(-jnp.abs(v)))


def _rms(v, w):
    return v * lax.rsqrt(jnp.mean(v * v, axis=-1, keepdims=True) + EPS) * w


def _dot(a, b):
    return jnp.dot(a, b, preferred_element_type=F32)


def _dot_nt(a, b):
    return lax.dot_general(a, b, (((1,), (1,)), ((), ())), preferred_element_type=F32)


def _lane_lo(shape):
    return lax.broadcasted_iota(jnp.int32, shape, len(shape) - 1) % LANES < HEAD


def _mod_kernel(cc_ref, w_ref, b_ref, o_ref):
    s = _silu(cc_ref[...])
    o_ref[0] = _dot(s.astype(MM), w_ref[0].astype(MM)) + b_ref[0]


def _modulation(c, c_ctx, ada_w, ada_b):
    depth, d, d3 = ada_w.shape
    cc = jnp.zeros((8, d), F32).at[0].set(c[0]).at[1].set(c_ctx)
    out = pl.pallas_call(
        _mod_kernel,
        grid=(depth, d3 // d),
        in_specs=[pl.BlockSpec((8, d), lambda i, j: (0, 0)),
                  pl.BlockSpec((1, d, d), lambda i, j: (i, 0, j)),
                  pl.BlockSpec((1, 1, d), lambda i, j: (i, 0, j))],
        out_specs=pl.BlockSpec((1, 8, d), lambda i, j: (i, 0, j)),
        out_shape=jax.ShapeDtypeStruct((depth, 8, d3), F32),
        compiler_params=_cparams("parallel", "parallel"),
        name="adaln_mod",
    )(cc, ada_w, ada_b.reshape(depth, 1, d3))
    return out


def _row_select(i, tm, n_lat, v2):
    rows = i * tm + lax.broadcasted_iota(jnp.int32, (tm, 1), 0)
    return jnp.where(rows < n_lat, v2[0:1, :], v2[1:2, :])


def _inproj_kernel(*refs, tm, n_lat, n_q, n_rope, q_scale):
    if n_rope:
        x_ref, nw_ref, sc_ref, sh_ref, w_ref, cos_ref, sin_ref, o_ref, h_ref = refs
    else:
        x_ref, nw_ref, sc_ref, sh_ref, w_ref, o_ref, h_ref = refs
    i, j = pl.program_id(0), pl.program_id(1)

    @pl.when(j == 0)
    def _():
        h = _rms(x_ref[...], nw_ref[...])
        h = h * (1.0 + _row_select(i, tm, n_lat, sc_ref[...])) + _row_select(i, tm, n_lat, sh_ref[...])
        h_ref[...] = h.astype(MM)

    acc = _dot(h_ref[...], w_ref[...])
    if n_q:
        acc = acc * jnp.where(j < n_q, q_scale, 1.0)
    if n_rope:
        tn = acc.shape[1]

        @pl.when(j < n_rope)
        def _():
            reps = tn // LANES
            cos = jnp.concatenate([cos_ref[...]] * reps, axis=1)
            sin = jnp.concatenate([sin_ref[...]] * reps, axis=1)
            lane = lax.broadcasted_iota(jnp.int32, acc.shape, 1)
            swapped = jnp.where(lane % HEAD < HEAD // 2,
                                pltpu.roll(acc, tn - HEAD // 2, 1), pltpu.roll(acc, HEAD // 2, 1))
            o_ref[...] = (acc * cos + swapped * sin).astype(o_ref.dtype)

        @pl.when(j >= n_rope)
        def _():
            o_ref[...] = acc.astype(o_ref.dtype)
    else:
        o_ref[...] = acc.astype(o_ref.dtype)


def _inproj(xa, nw, sc, sh, w, *, n_lat, out_dtype, tm, tn, n_q=0, n_rope=0, q_scale=1.0, rope=None,
            name="inproj"):
    t, d = xa.shape
    f = w.shape[1]
    assert t % tm == 0 and f % tn == 0
    in_specs = [pl.BlockSpec((tm, d), lambda i, j: (i, 0)),
                pl.BlockSpec((1, d), lambda i, j: (0, 0)),
                pl.BlockSpec((2, d), lambda i, j: (0, 0)),
                pl.BlockSpec((2, d), lambda i, j: (0, 0)),
                pl.BlockSpec((d, tn), lambda i, j: (0, j))]
    args = [xa, nw, sc, sh, w]
    if n_rope:
        in_specs += [pl.BlockSpec((tm, LANES), lambda i, j: (i, 0))] * 2
        args += list(rope)
    return pl.pallas_call(
        functools.partial(_inproj_kernel, tm=tm, n_lat=n_lat, n_q=n_q, n_rope=n_rope, q_scale=q_scale),
        grid=(t // tm, f // tn),
        in_specs=in_specs,
        out_specs=pl.BlockSpec((tm, tn), lambda i, j: (i, j)),
        out_shape=jax.ShapeDtypeStruct((t, f), out_dtype),
        scratch_shapes=[pltpu.VMEM((tm, d), MM)],
        compiler_params=_cparams("parallel", "arbitrary"),
        name=name,
    )(*args)


def _outproj_kernel(*refs, tm, n_lat, n_a, use_norm):
    a_refs = refs[:n_a]
    g_ref = refs[n_a]
    rest = refs[n_a + 1:]
    if use_norm:
        nm_ref, rest = rest[0], rest[1:]
    w_ref, x_ref, pw_ref, gate_ref, o_ref = rest
    i = pl.program_id(0)
    a = a_refs[0][...].astype(F32)
    for r in a_refs[1:]:
        a = a + r[...].astype(F32)
    a = a * _silu(g_ref[...])
    if use_norm:
        a = _rms(a, nm_ref[...])
    o = _dot(a.astype(MM), w_ref[...])
    o = _rms(o, pw_ref[...])
    o_ref[...] = x_ref[...] + _row_select(i, tm, n_lat, gate_ref[...]) * o


def _outproj(a_list, g, norm_w, w, xa, pw, gate, *, n_lat, tm, name="outproj"):
    t, d = xa.shape
    f = w.shape[0]
    assert t % tm == 0
    tok = lambda width: pl.BlockSpec((tm, width), lambda i: (i, 0))
    full = lambda r, c: pl.BlockSpec((r, c), lambda i: (0, 0))
    in_specs = [tok(f) for _ in a_list] + [tok(f)]
    args = list(a_list) + [g]
    if norm_w is not None:
        in_specs.append(full(1, f))
        args.append(norm_w)
    in_specs += [full(f, d), tok(d), full(1, d), full(2, d)]
    args += [w, xa, pw, gate]
    return pl.pallas_call(
        functools.partial(_outproj_kernel, tm=tm, n_lat=n_lat, n_a=len(a_list), use_norm=norm_w is not None),
        grid=(t // tm,),
        in_specs=in_specs,
        out_specs=tok(d),
        out_shape=jax.ShapeDtypeStruct((t, d), F32),
        compiler_params=_cparams("parallel"),
        name=name,
    )(*args)


def _conv_kernel(prev_ref, cur_ref, next_ref, w_ref, b_ref, o_ref, pad_ref, *, tm, first_blocks, last_blocks):
    i = pl.program_id(0)
    is_first = functools.reduce(jnp.logical_or, [i == b for b in first_blocks])
    is_last = functools.reduce(jnp.logical_or, [i == b for b in last_blocks])
    pad_ref[0:8, :] = jnp.where(is_first, 0.0, prev_ref[...])
    pad_ref[8:8 + tm, :] = cur_ref[...]
    pad_ref[8 + tm:16 + tm, :] = jnp.where(is_last, 0.0, next_ref[...])
    acc = jnp.zeros(cur_ref.shape, F32) + b_ref[...]
    half = SSD_CONV // 2
    for k in range(SSD_CONV):
        acc = acc + w_ref[k:k + 1, :] * pad_ref[8 - half + k:8 - half + k + tm, :]
    o_ref[...] = _silu(acc).astype(o_ref.dtype)


def _conv_silu(u, col0, width, conv_w, conv_b, *, n_lat, tm, tc, out_dtype, name):
    t = u.shape[0]
    assert t % tm == 0 and n_lat % tm == 0 and width % tc == 0 and col0 % tc == 0 and tm % 8 == 0
    cb0 = col0 // tc
    r8 = tm // 8
    nb8 = t // 8
    nl, nt = n_lat // tm, t // tm
    return pl.pallas_call(
        functools.partial(_conv_kernel, tm=tm, first_blocks=(0, nl), last_blocks=(nl - 1, nt - 1)),
        grid=(nt, width // tc),
        in_specs=[pl.BlockSpec((8, tc), lambda i, j: (jnp.maximum(i * r8 - 1, 0), cb0 + j)),
                  pl.BlockSpec((tm, tc), lambda i, j: (i, cb0 + j)),
                  pl.BlockSpec((8, tc), lambda i, j: (jnp.minimum((i + 1) * r8, nb8 - 1), cb0 + j)),
                  pl.BlockSpec((SSD_CONV, tc), lambda i, j: (0, j)),
                  pl.BlockSpec((1, tc), lambda i, j: (0, j))],
        out_specs=pl.BlockSpec((tm, tc), lambda i, j: (i, j)),
        out_shape=jax.ShapeDtypeStruct((t, width), out_dtype),
        scratch_shapes=[pltpu.VMEM((tm + 16, tc), F32)],
        compiler_params=_cparams("parallel", "parallel"),
        name=name,
    )(u, u, u, conv_w, conv_b)


def _ssd_kernel(x_ref, b_ref, c_ref, dt_ref, bias_ref, alog_ref, dskip_ref, y_ref, h_ref):
    d, s = pl.program_id(0), pl.program_id(1)
    q = SSD_CHUNK

    @pl.when(s == 0)
    def _():
        h_ref[...] = jnp.zeros(h_ref.shape, F32)

    dt = _softplus(dt_ref[...] + bias_ref[...])
    a = dt * (-jnp.exp(alog_ref[...]))
    ii = lax.broadcasted_iota(jnp.int32, (q, q), 0)
    jj = lax.broadcasted_iota(jnp.int32, (q, q), 1)
    keep = jnp.where(d == 0, jj - ii, ii - jj) <= 0
    cs = jnp.dot(keep.astype(F32), a, preferred_element_type=F32, precision=lax.Precision.HIGHEST)
    cs_t = cs.T
    dt_t = dt.T
    tot = jnp.sum(a, axis=0, keepdims=True)
    tot_t = jnp.sum(a.T, axis=1, keepdims=True)
    w_t = jnp.exp(tot_t - cs_t) * dt_t
    dec = jnp.exp(tot)
    lane_lo = _lane_lo((q, LANES))
    skip_on = jnp.where(d == 0, 1.0, 0.0)

    n_groups = b_ref.shape[1] // SSD_STATE
    hpg = SSD_HEADS // n_groups
    for g in range(n_groups):
        bg = b_ref[:, g * SSD_STATE:(g + 1) * SSD_STATE]
        cg = c_ref[:, g * SSD_STATE:(g + 1) * SSD_STATE]
        cb = _dot_nt(cg, bg)
        bg_t = bg.astype(F32).T
        cg32 = cg.astype(F32)
        for pr in range(hpg // 2):
            p = g * (hpg // 2) + pr
            x32 = x_ref[:, p * LANES:(p + 1) * LANES]
            xb = x32.astype(MM)
            hp = h_ref[p]
            lhs_y, lhs_h = [], []
            for half in range(2):
                hd = 2 * p + half
                col = jnp.broadcast_to(cs[:, hd:hd + 1], (q, q))
                row = jnp.broadcast_to(cs_t[hd:hd + 1, :], (q, q))
                lmat = jnp.exp(jnp.where(keep, col - row, NEG))
                m = cb * lmat * dt_t[hd:hd + 1, :]
                ce = cg32 * jnp.exp(col)
                lhs_y.append(jnp.concatenate([m.astype(MM), ce.astype(MM)], axis=1))
                lhs_h.append((bg_t * w_t[hd:hd + 1, :]).astype(MM))
            rhs = jnp.concatenate([xb, hp.astype(MM)], axis=0)
            yf = _dot(jnp.concatenate(lhs_y, axis=0), rhs)
            hf = _dot(jnp.concatenate(lhs_h, axis=0), xb)
            y = jnp.where(lane_lo, yf[:q], yf[q:])
            y_ref[0, :, p * LANES:(p + 1) * LANES] = y + skip_on * dskip_ref[:, p * LANES:(p + 1) * LANES] * x32
            hn = jnp.where(lane_lo, hf[:SSD_STATE], hf[SSD_STATE:])
            hd0 = 2 * p
            dec_p = jnp.where(lane_lo[0:1, :], dec[:, hd0:hd0 + 1], dec[:, hd0 + 1:hd0 + 2])
            h_ref[p] = hp * dec_p + hn


def _ssd_scan(xs, bc, dtp, dt_bias, a_log, d_skip, *, n_lat):
    t, inner = xs.shape
    gn = bc.shape[1] // 2
    q = SSD_CHUNK
    nlc, ntc = n_lat // q, t // q

    def chunk(d, s):
        fwd = jnp.where(s < ntc - nlc, nlc + s, s - (ntc - nlc))
        return jnp.where(d == 0, fwd, ntc - 1 - s)

    pad = lambda v: jnp.zeros((2, 1, LANES), F32).at[:, 0, :SSD_HEADS].set(v)
    dsk = jnp.repeat(d_skip, inner // SSD_HEADS)[None, :]
    return pl.pallas_call(
        _ssd_kernel,
        grid=(2, ntc),
        in_specs=[pl.BlockSpec((q, inner), lambda d, s: (chunk(d, s), 0)),
                  pl.BlockSpec((q, gn), lambda d, s: (chunk(d, s), 0)),
                  pl.BlockSpec((q, gn), lambda d, s: (chunk(d, s), 1)),
                  pl.BlockSpec((q, LANES), lambda d, s: (chunk(d, s), d)),
                  pl.BlockSpec((None, 1, LANES), lambda d, s: (d, 0, 0)),
                  pl.BlockSpec((None, 1, LANES), lambda d, s: (d, 0, 0)),
                  pl.BlockSpec((1, inner), lambda d, s: (0, 0))],
        out_specs=pl.BlockSpec((1, q, inner), lambda d, s: (d, chunk(d, s), 0)),
        out_shape=jax.ShapeDtypeStruct((2, t, inner), F32),
        scratch_shapes=[pltpu.VMEM((SSD_HEADS // 2, SSD_STATE, LANES), F32)],
        compiler_params=_cparams("parallel", "arbitrary"),
        name="ssd_scan",
    )(xs, bc, bc, dtp, pad(dt_bias), pad(a_log), dsk)


def _diff_kernel(*refs, stream, lam_init):
    if stream:
        q_ref, kc_ref, vc_ref, k_ref, v_ref, lam_ref, sub_ref, o_ref, m_ref, l_ref, acc_ref = refs
    else:
        q_ref, kc_ref, vc_ref, lam_ref, sub_ref, o_ref, m_ref, l_ref, acc_ref = refs
    j = pl.program_id(2)
    nj = pl.num_programs(2)
    q = q_ref[...]
    lo = _lane_lo(q.shape)
    zero = jnp.zeros_like(q)
    qs = (jnp.where(lo, q, zero), jnp.where(lo, zero, q))

    def update(k, v):
        for i in range(2):
            s = _dot_nt(qs[i], k)
            m_old = m_ref[i]
            m_new = jnp.maximum(m_old, jnp.max(s, axis=1, keepdims=True))
            alpha = jnp.exp(m_old - m_new)
            p = jnp.exp(s - m_new)
            l_ref[i] = alpha * l_ref[i] + jnp.sum(p, axis=1, keepdims=True)
            acc_ref[i] = alpha * acc_ref[i] + _dot(p.astype(MM), v)
            m_ref[i] = m_new

    @pl.when(j == 0)
    def _():
        m_ref[...] = jnp.full(m_ref.shape, NEG, F32)
        l_ref[...] = jnp.zeros(l_ref.shape, F32)
        acc_ref[...] = jnp.zeros(acc_ref.shape, F32)
        update(kc_ref[...], vc_ref[...])

    if stream:
        update(k_ref[...], v_ref[...])

    @pl.when(j == nj - 1)
    def _():
        lam4 = lam_ref[...]
        lam = (jnp.exp(jnp.sum(lam4[0:1] * lam4[1:2], axis=1, keepdims=True))
               - jnp.exp(jnp.sum(lam4[2:3] * lam4[3:4], axis=1, keepdims=True)) + lam_init)
        o = acc_ref[0] / l_ref[0] - lam * (acc_ref[1] / l_ref[1])
        o_ref[...] = _rms(o, sub_ref[...]) * (1.0 - lam_init)


def _diff_attention(qkv, lam4, subln, *, n_lat, lam_init, tq, tk):
    t = qkv.shape[0]
    nc = t - n_lat
    width = qkv.shape[1] // 3
    nh = width // LANES
    ctx_blk = n_lat // nc

    def call(stream, q_rows, q_blk0, tq_, nk):
        in_specs = [pl.BlockSpec((tq_, LANES), lambda h, i, j: (q_blk0 + i, h)),
                    pl.BlockSpec((nc, LANES), lambda h, i, j: (ctx_blk, nh + h)),
                    pl.BlockSpec((nc, LANES), lambda h, i, j: (ctx_blk, 2 * nh + h))]
        args = [qkv, qkv, qkv]
        if stream:
            in_specs += [pl.BlockSpec((tk, LANES), lambda h, i, j: (j, nh + h)),
                         pl.BlockSpec((tk, LANES), lambda h, i, j: (j, 2 * nh + h))]
            args += [qkv, qkv]
        in_specs += [pl.BlockSpec((4, HEAD), lambda h, i, j: (0, 0)),
                     pl.BlockSpec((1, LANES), lambda h, i, j: (0, 0))]
        args += [lam4, subln]
        return pl.pallas_call(
            functools.partial(_diff_kernel, stream=stream, lam_init=lam_init),
            grid=(nh, q_rows // tq_, nk),
            in_specs=in_specs,
            out_specs=pl.BlockSpec((tq_, LANES), lambda h, i, j: (i, h)),
            out_shape=jax.ShapeDtypeStruct((q_rows, width), F32),
            scratch_shapes=[pltpu.VMEM((2, tq_, 1), F32), pltpu.VMEM((2, tq_, 1), F32),
                            pltpu.VMEM((2, tq_, LANES), F32)],
            compiler_params=_cparams("parallel", "parallel", "arbitrary"),
            name="diff_attn" if stream else "diff_attn_ctx",
        )(*args)

    assert n_lat % tq == 0 and n_lat % tk == 0 and n_lat % nc == 0
    o_lat = call(True, n_lat, 0, tq, n_lat // tk)
    o_ctx = call(False, nc, ctx_blk, nc, 1)
    return jnp.concatenate([o_lat, o_ctx], axis=0)


def _na_kernel(q_ref, kb_ref, vb_ref, kc_ref, vc_ref, bias_ref, o_ref):
    n_pairs = q_ref.shape[1] // LANES
    lo = _lane_lo((q_ref.shape[0], LANES))
    for p in range(n_pairs):
        sl = slice(p * LANES, (p + 1) * LANES)
        q = q_ref[:, sl]
        kb, vb, kc, vc = kb_ref[:, sl], vb_ref[:, sl], kc_ref[:, sl], vc_ref[:, sl]
        zero = jnp.zeros_like(q)
        outs = []
        for half in range(2):
            qh = jnp.where(lo, q, zero) if half == 0 else jnp.where(lo, zero, q)
            s_loc = _dot_nt(qh, kb) + bias_ref[2 * p + half]
            s_ctx = _dot_nt(qh, kc)
            m = jnp.maximum(jnp.max(s_loc, axis=1, keepdims=True), jnp.max(s_ctx, axis=1, keepdims=True))
            p_loc = jnp.exp(s_loc - m)
            p_ctx = jnp.exp(s_ctx - m)
            l = jnp.sum(p_loc, axis=1, keepdims=True) + jnp.sum(p_ctx, axis=1, keepdims=True)
            outs.append((_dot(p_loc.astype(MM), vb) + _dot(p_ctx.astype(MM), vc)) / l)
        o_ref[:, sl] = jnp.where(lo, outs[0], outs[1])


def _na_ctx_kernel(q_ref, k_ref, v_ref, o_ref):
    q, k, v = q_ref[...], k_ref[...], v_ref[...]
    lo = _lane_lo(q.shape)
    zero = jnp.zeros_like(q)
    outs = []
    for half in range(2):
        qh = jnp.where(lo, q, zero) if half == 0 else jnp.where(lo, zero, q)
        s = _dot_nt(qh, k)
        p = jnp.exp(s - jnp.max(s, axis=1, keepdims=True))
        outs.append(_dot(p.astype(MM), v) / jnp.sum(p, axis=1, keepdims=True))
    o_ref[...] = jnp.where(lo, outs[0], outs[1])


def _na_bias_table(rpb, rows):
    kr = min(NA_KR, rows)
    kw = min(NA_KC, GRID_W)
    w = np.arange(GRID_W)
    c0 = np.clip(w - kw // 2, 0, GRID_W - kw)
    kc = np.arange(GRID_W)
    valid = (kc[None, :] >= c0[:, None]) & (kc[None, :] < c0[:, None] + kw)
    col = np.clip(kc[None, :] - w[:, None] + (NA_KC - 1), 0, 2 * NA_KC - 2)
    row = np.arange(NA_KR)[:, None] + np.arange(kr)[None, :]
    tab = rpb[:, row][:, :, :, col]
    tab = jnp.where(valid[None, None, None], tab, NEG)
    tab = tab.transpose(1, 0, 3, 2, 4)
    return tab.reshape(NA_KR, rpb.shape[0], GRID_W, kr * GRID_W).astype(F32)


def _na_attention(qkv, rpb, *, n_lat):
    t = qkv.shape[0]
    nc = t - n_lat
    width = qkv.shape[1] // 3
    rows = n_lat // GRID_W
    kr = min(NA_KR, rows)
    band = kr * GRID_W
    ctx_blk = n_lat // nc
    table = _na_bias_table(rpb, rows)

    def r0(r):
        return jnp.clip(r - kr // 2, 0, rows - kr)

    o_lat = pl.pallas_call(
        _na_kernel,
        grid=(rows,),
        in_specs=[pl.BlockSpec((GRID_W, width), lambda r: (r, 0)),
                  pl.BlockSpec((pl.Element(band), pl.Element(width)), lambda r: (r0(r) * GRID_W, width)),
                  pl.BlockSpec((pl.Element(band), pl.Element(width)), lambda r: (r0(r) * GRID_W, 2 * width)),
                  pl.BlockSpec((nc, width), lambda r: (ctx_blk, 1)),
                  pl.BlockSpec((nc, width), lambda r: (ctx_blk, 2)),
                  pl.BlockSpec((None, rpb.shape[0], GRID_W, band), lambda r: (r0(r) - r + NA_KR - 1, 0, 0, 0))],
        out_specs=pl.BlockSpec((GRID_W, width), lambda r: (r, 0)),
        out_shape=jax.ShapeDtypeStruct((n_lat, width), F32),
        compiler_params=_cparams("parallel"),
        name="na_attn",
    )(qkv, qkv, qkv, qkv, qkv, table)

    nh = width // LANES
    o_ctx = pl.pallas_call(
        _na_ctx_kernel,
        grid=(nh,),
        in_specs=[pl.BlockSpec((nc, LANES), lambda h: (ctx_blk, h)),
                  pl.BlockSpec((nc, LANES), lambda h: (ctx_blk, nh + h)),
                  pl.BlockSpec((nc, LANES), lambda h: (ctx_blk, 2 * nh + h))],
        out_specs=pl.BlockSpec((nc, LANES), lambda h: (0, h)),
        out_shape=jax.ShapeDtypeStruct((nc, width), F32),
        compiler_params=_cparams("parallel"),
        name="na_attn_ctx",
    )(qkv, qkv, qkv)
    return jnp.concatenate([o_lat, o_ctx], axis=0)


def _rope_tables(n_lat, n_ctx):
    tok = jnp.arange(n_lat)
    row = (tok // GRID_W).astype(F32)
    col = (tok % GRID_W).astype(F32)
    n_freq = HEAD // 4
    inv = ROPE_BASE ** (-jnp.arange(n_freq, dtype=F32) / n_freq)
    ang = jnp.concatenate([row[:, None] * inv, col[:, None] * inv], axis=-1)
    cos, sin = jnp.cos(ang), jnp.sin(ang)
    cos = jnp.concatenate([jnp.tile(cos, (1, 4)), jnp.ones((n_ctx, LANES), F32)], axis=0)
    sin = jnp.concatenate([jnp.concatenate([-sin, sin, -sin, sin], axis=1), jnp.zeros((n_ctx, LANES), F32)], axis=0)
    return cos, sin


def _pick_tm(t, cap):
    return max(m for m in range(8, cap + 1, 8) if t % m == 0)


def kernel(x, c, ctx, c_ctx, ada_w, ada_b, norm_pre, norm_post, ssd_w_in, ssd_conv_w, ssd_conv_b, ssd_dt_bias, ssd_a_log, ssd_d, ssd_norm, ssd_w_out, dif_w_in, dif_lam_q1, dif_lam_k1, dif_lam_q2, dif_lam_k2, dif_subln, dif_w_out, na_w_in, na_rpb, na_w_out):
    n_lat, d = x.shape[1], x.shape[2]
    n_ctx = ctx.shape[1]
    depth = ada_w.shape[0]
    xa = jnp.concatenate([x[0], ctx[0]], axis=0)
    t = xa.shape[0]
    mods = _modulation(c, c_ctx, ada_w, ada_b)
    rope = None
    tm_in = _pick_tm(t, 1280)
    tm_out = _pick_tm(t, 640)
    ia = ib = ic = 0
    for i in range(depth):
        shift, scale, gate = (mods[i, 0:2, k * d:(k + 1) * d] for k in range(3))
        nw = norm_pre[i][None, :]
        pw = norm_post[i][None, :]
        inproj = functools.partial(_inproj, xa, nw, scale, shift, n_lat=n_lat, tm=tm_in)
        kind = i % 3
        if kind == 0:
            w_in = ssd_w_in[ia]
            inner = ssd_w_out.shape[1]
            gn2 = w_in.shape[1] - 2 * inner - 2 * SSD_HEADS
            u = inproj(w_in[:, :2 * inner + gn2].astype(MM), out_dtype=F32, tn=512, name="ssd_inproj")
            w_dt = jnp.zeros((d, 2 * LANES), F32)
            w_dt = w_dt.at[:, :SSD_HEADS].set(w_in[:, 2 * inner + gn2:2 * inner + gn2 + SSD_HEADS])
            w_dt = w_dt.at[:, LANES:LANES + SSD_HEADS].set(w_in[:, 2 * inner + gn2 + SSD_HEADS:])
            dtp = inproj(w_dt.astype(MM), out_dtype=F32, tn=2 * LANES, name="ssd_inproj_dt")
            cw, cb = ssd_conv_w[ia], ssd_conv_b[ia][None, :]
            xs = _conv_silu(u, inner, inner, cw[:, :inner], cb[:, :inner], n_lat=n_lat, tm=n_ctx, tc=512,
                            out_dtype=F32, name="ssd_conv_x")
            bc = _conv_silu(u, 2 * inner, gn2, cw[:, inner:], cb[:, inner:], n_lat=n_lat, tm=n_ctx, tc=512,
                            out_dtype=MM, name="ssd_conv_bc")
            y = _ssd_scan(xs, bc, dtp, ssd_dt_bias[ia], ssd_a_log[ia], ssd_d[ia], n_lat=n_lat)
            z = u[:, :inner]
            xa = _outproj([y[0], y[1]], z, ssd_norm[ia][None, :], ssd_w_out[ia].astype(MM), xa, pw, gate,
                          n_lat=n_lat, tm=tm_out, name="ssd_outproj")
            ia += 1
        elif kind == 1:
            lam_init = 0.8 - 0.6 * math.exp(-0.3 * i)
            w_in = dif_w_in[ib]
            width = dif_w_out.shape[1]
            if rope is None:
                rope = _rope_tables(n_lat, n_ctx)
            qkv = inproj(w_in[:, :3 * width].astype(MM), out_dtype=MM, tn=256, n_q=width // 256,
                         n_rope=2 * width // 256, q_scale=HEAD ** -0.5, rope=rope, name="dif_inproj_qkv")
            g = inproj(w_in[:, 3 * width:].astype(MM), out_dtype=F32, tn=512, name="dif_inproj_g")
            lam4 = jnp.stack([dif_lam_q1[ib], dif_lam_k1[ib], dif_lam_q2[ib], dif_lam_k2[ib]])
            o = _diff_attention(qkv, lam4, dif_subln[ib][None, :], n_lat=n_lat, lam_init=lam_init,
                                tq=min(512, n_lat), tk=min(1024, n_lat))
            xa = _outproj([o], g, None, dif_w_out[ib].astype(MM), xa, pw, gate, n_lat=n_lat, tm=tm_out,
                          name="dif_outproj")
            ib += 1
        else:
            w_in = na_w_in[ic]
            qkv = inproj(w_in[:, :3 * d].astype(MM), out_dtype=MM, tn=256, n_q=d // 256,
                         q_scale=HEAD ** -0.5, name="na_inproj_qkv")
            g = inproj(w_in[:, 3 * d:].astype(MM), out_dtype=F32, tn=512, name="na_inproj_g")
            o = _na_attention(qkv, na_rpb[ic], n_lat=n_lat)
            xa = _outproj([o], g, None, na_w_out[ic].astype(MM), xa, pw, gate, n_lat=n_lat, tm=tm_out,
                          name="na_outproj")
            ic += 1
    return xa[:n_lat][None]
```

```python
import functools
import math

import numpy as np
import jax
import jax.numpy as jnp
from jax import lax
from jax.experimental import pallas as pl
from jax.experimental.pallas import tpu as pltpu

F32 = jnp.float32
MM = jnp.bfloat16

EPS = 1e-6
GRID_W = 64
LANES = 128
HEAD = 64
NEG = -1e30

SSD_HEADS = 32
SSD_STATE = 128
SSD_CONV = 5
SSD_CHUNK = 128

NA_KR = 8
NA_KC = 16
ROPE_BASE = 10000.0

VMEM_LIMIT = 52 * 1024 * 1024


def _cparams(*sem):
    return pltpu.CompilerParams(dimension_semantics=sem, vmem_limit_bytes=VMEM_LIMIT)


def _silu(v):
    return v * (1.0 / (1.0 + jnp.exp(-v)))


def _softplus(v):
    return jnp.maximum(v, 0.0) + jnp.log(1.0 + jnp.exp(-jnp.abs(v)))


def _rms(v, w):
    return v * lax.rsqrt(jnp.mean(v * v, axis=-1, keepdims=True) + EPS) * w


def _dot(a, b):
    return jnp.dot(a, b, preferred_element_type=F32)


def _dot_nt(a, b):
    return lax.dot_general(a, b, (((1,), (1,)), ((), ())), preferred_element_type=F32)


def _lane_lo(shape):
    return lax.broadcasted_iota(jnp.int32, shape, len(shape) - 1) % LANES < HEAD


def _mod_kernel(cc_ref, w_ref, b_ref, o_ref):
    s = _silu(cc_ref[...])
    o_ref[0] = _dot(s.astype(MM), w_ref[0].astype(MM)) + b_ref[0]


def _modulation(c, c_ctx, ada_w, ada_b):
    depth, d, d3 = ada_w.shape
    cc = jnp.zeros((8, d), F32).at[0].set(c[0]).at[1].set(c_ctx)
    out = pl.pallas_call(
        _mod_kernel,
        grid=(depth, d3 // d),
        in_specs=[pl.BlockSpec((8, d), lambda i, j: (0, 0)),
                  pl.BlockSpec((1, d, d), lambda i, j: (i, 0, j)),
                  pl.BlockSpec((1, 1, d), lambda i, j: (i, 0, j))],
        out_specs=pl.BlockSpec((1, 8, d), lambda i, j: (i, 0, j)),
        out_shape=jax.ShapeDtypeStruct((depth, 8, d3), F32),
        compiler_params=_cparams("parallel", "parallel"),
        name="adaln_mod",
    )(cc, ada_w, ada_b.reshape(depth, 1, d3))
    return out


def _row_select(i, tm, n_lat, v2):
    rows = i * tm + lax.broadcasted_iota(jnp.int32, (tm, 1), 0)
    return jnp.where(rows < n_lat, v2[0:1, :], v2[1:2, :])


def _inproj_kernel(*refs, tm, n_lat, n_q, n_rope, q_scale, transposed):
    if n_rope:
        x_ref, nw_ref, sc_ref, sh_ref, w_ref, cos_ref, sin_ref, o_ref, h_ref = refs
    else:
        x_ref, nw_ref, sc_ref, sh_ref, w_ref, o_ref, h_ref = refs
    i, j = pl.program_id(0), pl.program_id(1)
    fa = 0 if transposed else 1

    @pl.when(j == 0)
    def _():
        h = _rms(x_ref[...], nw_ref[...])
        h = h * (1.0 + _row_select(i, tm, n_lat, sc_ref[...])) + _row_select(i, tm, n_lat, sh_ref[...])
        h_ref[...] = h.astype(MM)

    acc = _dot_nt(w_ref[...], h_ref[...]) if transposed else _dot(h_ref[...], w_ref[...])
    if n_q:
        acc = acc * jnp.where(j < n_q, q_scale, 1.0)
    if n_rope:
        tn = acc.shape[fa]

        @pl.when(j < n_rope)
        def _():
            reps = tn // LANES
            cos = jnp.concatenate([cos_ref[...]] * reps, axis=fa)
            sin = jnp.concatenate([sin_ref[...]] * reps, axis=fa)
            feat = lax.broadcasted_iota(jnp.int32, acc.shape, fa)
            swapped = jnp.where(feat % HEAD < HEAD // 2,
                                pltpu.roll(acc, tn - HEAD // 2, fa), pltpu.roll(acc, HEAD // 2, fa))
            o_ref[...] = (acc * cos + swapped * sin).astype(o_ref.dtype)

        @pl.when(j >= n_rope)
        def _():
            o_ref[...] = acc.astype(o_ref.dtype)
    else:
        o_ref[...] = acc.astype(o_ref.dtype)


def _inproj(xa, nw, sc, sh, w, *, n_lat, out_dtype, tm, tn, n_q=0, n_rope=0, q_scale=1.0, rope=None,
            transposed=False, name="inproj"):
    t, d = xa.shape
    f = w.shape[0] if transposed else w.shape[1]
    assert t % tm == 0 and f % tn == 0
    in_specs = [pl.BlockSpec((tm, d), lambda i, j: (i, 0)),
                pl.BlockSpec((1, d), lambda i, j: (0, 0)),
                pl.BlockSpec((2, d), lambda i, j: (0, 0)),
                pl.BlockSpec((2, d), lambda i, j: (0, 0)),
                pl.BlockSpec((tn, d), lambda i, j: (j, 0)) if transposed else pl.BlockSpec((d, tn), lambda i, j: (0, j))]
    args = [xa, nw, sc, sh, w]
    if n_rope:
        tab = pl.BlockSpec((LANES, tm), lambda i, j: (0, i)) if transposed else pl.BlockSpec((tm, LANES), lambda i, j: (i, 0))
        in_specs += [tab, tab]
        args += list(rope)
    if transposed:
        out_spec, out_shape = pl.BlockSpec((tn, tm), lambda i, j: (j, i)), (f, t)
    else:
        out_spec, out_shape = pl.BlockSpec((tm, tn), lambda i, j: (i, j)), (t, f)
    return pl.pallas_call(
        functools.partial(_inproj_kernel, tm=tm, n_lat=n_lat, n_q=n_q, n_rope=n_rope, q_scale=q_scale,
                          transposed=transposed),
        grid=(t // tm, f // tn),
        in_specs=in_specs,
        out_specs=out_spec,
        out_shape=jax.ShapeDtypeStruct(out_shape, out_dtype),
        scratch_shapes=[pltpu.VMEM((tm, d), MM)],
        compiler_params=_cparams("parallel", "arbitrary"),
        name=name,
    )(*args)


def _outproj_kernel(*refs, tm, n_lat, n_a, use_norm):
    a_refs = refs[:n_a]
    g_ref = refs[n_a]
    rest = refs[n_a + 1:]
    if use_norm:
        nm_ref, rest = rest[0], rest[1:]
    w_ref, x_ref, pw_ref, gate_ref, o_ref = rest
    i = pl.program_id(0)
    a = a_refs[0][...].astype(F32)
    for r in a_refs[1:]:
        a = a + r[...].astype(F32)
    a = a * _silu(g_ref[...])
    if use_norm:
        a = _rms(a, nm_ref[...])
    o = _dot(a.astype(MM), w_ref[...])
    o = _rms(o, pw_ref[...])
    o_ref[...] = x_ref[...] + _row_select(i, tm, n_lat, gate_ref[...]) * o


def _outproj(a_list, g, norm_w, w, xa, pw, gate, *, n_lat, tm, name="outproj"):
    t, d = xa.shape
    f = w.shape[0]
    assert t % tm == 0
    tok = lambda width: pl.BlockSpec((tm, width), lambda i: (i, 0))
    full = lambda r, c: pl.BlockSpec((r, c), lambda i: (0, 0))
    in_specs = [tok(f) for _ in a_list] + [tok(f)]
    args = list(a_list) + [g]
    if norm_w is not None:
        in_specs.append(full(1, f))
        args.append(norm_w)
    in_specs += [full(f, d), tok(d), full(1, d), full(2, d)]
    args += [w, xa, pw, gate]
    return pl.pallas_call(
        functools.partial(_outproj_kernel, tm=tm, n_lat=n_lat, n_a=len(a_list), use_norm=norm_w is not None),
        grid=(t // tm,),
        in_specs=in_specs,
        out_specs=tok(d),
        out_shape=jax.ShapeDtypeStruct((t, d), F32),
        compiler_params=_cparams("parallel"),
        name=name,
    )(*args)


def _conv_kernel(prev_ref, cur_ref, next_ref, w_ref, b_ref, o_ref, pad_ref, *, tm, first_blocks, last_blocks):
    i = pl.program_id(0)
    is_first = functools.reduce(jnp.logical_or, [i == b for b in first_blocks])
    is_last = functools.reduce(jnp.logical_or, [i == b for b in last_blocks])
    pad_ref[0:8, :] = jnp.where(is_first, 0.0, prev_ref[...])
    pad_ref[8:8 + tm, :] = cur_ref[...]
    pad_ref[8 + tm:16 + tm, :] = jnp.where(is_last, 0.0, next_ref[...])
    acc = jnp.zeros(cur_ref.shape, F32) + b_ref[...]
    half = SSD_CONV // 2
    for k in range(SSD_CONV):
        acc = acc + w_ref[k:k + 1, :] * pad_ref[8 - half + k:8 - half + k + tm, :]
    o_ref[...] = _silu(acc).astype(o_ref.dtype)


def _conv_silu(u, col0, width, conv_w, conv_b, *, n_lat, tm, tc, out_dtype, name):
    t = u.shape[0]
    assert t % tm == 0 and n_lat % tm == 0 and width % tc == 0 and col0 % tc == 0 and tm % 8 == 0
    cb0 = col0 // tc
    r8 = tm // 8
    nb8 = t // 8
    nl, nt = n_lat // tm, t // tm
    return pl.pallas_call(
        functools.partial(_conv_kernel, tm=tm, first_blocks=(0, nl), last_blocks=(nl - 1, nt - 1)),
        grid=(nt, width // tc),
        in_specs=[pl.BlockSpec((8, tc), lambda i, j: (jnp.maximum(i * r8 - 1, 0), cb0 + j)),
                  pl.BlockSpec((tm, tc), lambda i, j: (i, cb0 + j)),
                  pl.BlockSpec((8, tc), lambda i, j: (jnp.minimum((i + 1) * r8, nb8 - 1), cb0 + j)),
                  pl.BlockSpec((SSD_CONV, tc), lambda i, j: (0, j)),
                  pl.BlockSpec((1, tc), lambda i, j: (0, j))],
        out_specs=pl.BlockSpec((tm, tc), lambda i, j: (i, j)),
        out_shape=jax.ShapeDtypeStruct((t, width), out_dtype),
        scratch_shapes=[pltpu.VMEM((tm + 16, tc), F32)],
        compiler_params=_cparams("parallel", "parallel"),
        name=name,
    )(u, u, u, conv_w, conv_b)


def _ssd_kernel(x_ref, b_ref, c_ref, dt_ref, bias_ref, alog_ref, dskip_ref, y_ref, h_ref):
    d, s = pl.program_id(0), pl.program_id(1)
    q = SSD_CHUNK

    @pl.when(s == 0)
    def _():
        h_ref[...] = jnp.zeros(h_ref.shape, F32)

    dt = _softplus(dt_ref[...] + bias_ref[...])
    a = dt * (-jnp.exp(alog_ref[...]))
    ii = lax.broadcasted_iota(jnp.int32, (q, q), 0)
    jj = lax.broadcasted_iota(jnp.int32, (q, q), 1)
    keep = jnp.where(d == 0, jj - ii, ii - jj) <= 0
    cs = jnp.dot(keep.astype(F32), a, preferred_element_type=F32, precision=lax.Precision.HIGHEST)
    cs_t = cs.T
    dt_t = dt.T
    tot = jnp.sum(a, axis=0, keepdims=True)
    tot_t = jnp.sum(a.T, axis=1, keepdims=True)
    w_t = jnp.exp(tot_t - cs_t) * dt_t
    dec = jnp.exp(tot)
    lane_lo = _lane_lo((q, LANES))
    skip_on = jnp.where(d == 0, 1.0, 0.0)

    n_groups = b_ref.shape[1] // SSD_STATE
    hpg = SSD_HEADS // n_groups
    for g in range(n_groups):
        bg = b_ref[:, g * SSD_STATE:(g + 1) * SSD_STATE]
        cg = c_ref[:, g * SSD_STATE:(g + 1) * SSD_STATE]
        cb = _dot_nt(cg, bg)
        bg_t = bg.astype(F32).T
        cg32 = cg.astype(F32)
        for pr in range(hpg // 2):
            p = g * (hpg // 2) + pr
            x32 = x_ref[:, p * LANES:(p + 1) * LANES]
            xb = x32.astype(MM)
            hp = h_ref[p]
            lhs_y, lhs_h = [], []
            for half in range(2):
                hd = 2 * p + half
                col = jnp.broadcast_to(cs[:, hd:hd + 1], (q, q))
                row = jnp.broadcast_to(cs_t[hd:hd + 1, :], (q, q))
                lmat = jnp.exp(jnp.where(keep, col - row, NEG))
                m = cb * lmat * dt_t[hd:hd + 1, :]
                ce = cg32 * jnp.exp(col)
                lhs_y.append(jnp.concatenate([m.astype(MM), ce.astype(MM)], axis=1))
                lhs_h.append((bg_t * w_t[hd:hd + 1, :]).astype(MM))
            rhs = jnp.concatenate([xb, hp.astype(MM)], axis=0)
            yf = _dot(jnp.concatenate(lhs_y, axis=0), rhs)
            hf = _dot(jnp.concatenate(lhs_h, axis=0), xb)
            y = jnp.where(lane_lo, yf[:q], yf[q:])
            y_ref[0, :, p * LANES:(p + 1) * LANES] = y + skip_on * dskip_ref[:, p * LANES:(p + 1) * LANES] * x32
            hn = jnp.where(lane_lo, hf[:SSD_STATE], hf[SSD_STATE:])
            hd0 = 2 * p
            dec_p = jnp.where(lane_lo[0:1, :], dec[:, hd0:hd0 + 1], dec[:, hd0 + 1:hd0 + 2])
            h_ref[p] = hp * dec_p + hn


def _ssd_scan(xs, bc, dtp, dt_bias, a_log, d_skip, *, n_lat):
    t, inner = xs.shape
    gn = bc.shape[1] // 2
    q = SSD_CHUNK
    nlc, ntc = n_lat // q, t // q

    def chunk(d, s):
        fwd = jnp.where(s < ntc - nlc, nlc + s, s - (ntc - nlc))
        return jnp.where(d == 0, fwd, ntc - 1 - s)

    pad = lambda v: jnp.zeros((2, 1, LANES), F32).at[:, 0, :SSD_HEADS].set(v)
    dsk = jnp.repeat(d_skip, inner // SSD_HEADS)[None, :]
    return pl.pallas_call(
        _ssd_kernel,
        grid=(2, ntc),
        in_specs=[pl.BlockSpec((q, inner), lambda d, s: (chunk(d, s), 0)),
                  pl.BlockSpec((q, gn), lambda d, s: (chunk(d, s), 0)),
                  pl.BlockSpec((q, gn), lambda d, s: (chunk(d, s), 1)),
                  pl.BlockSpec((q, LANES), lambda d, s: (chunk(d, s), d)),
                  pl.BlockSpec((None, 1, LANES), lambda d, s: (d, 0, 0)),
                  pl.BlockSpec((None, 1, LANES), lambda d, s: (d, 0, 0)),
                  pl.BlockSpec((1, inner), lambda d, s: (0, 0))],
        out_specs=pl.BlockSpec((1, q, inner), lambda d, s: (d, chunk(d, s), 0)),
        out_shape=jax.ShapeDtypeStruct((2, t, inner), F32),
        scratch_shapes=[pltpu.VMEM((SSD_HEADS // 2, SSD_STATE, LANES), F32)],
        compiler_params=_cparams("parallel", "arbitrary"),
        name="ssd_scan",
    )(xs, bc, bc, dtp, pad(dt_bias), pad(a_log), dsk)


def _diff_kernel(*refs, stream, lam_init):
    if stream:
        (qt_ref, kc_ref, vct_ref, k_ref, vt_ref, lam_ref, sub_ref, o_ref,
         qs_ref, m_ref, l_ref, acc_ref, s0_ref, s1_ref, mt0_ref, mt1_ref) = refs
    else:
        qt_ref, kc_ref, vct_ref, lam_ref, sub_ref, o_ref, qs_ref, m_ref, l_ref, acc_ref = refs
    j = pl.program_id(2)
    nj = pl.num_programs(2)

    def scores(k, s_ref, mt_ref):
        for i in range(2):
            s = _dot(k, qs_ref[i])
            s_ref[i] = s
            mt_ref[i] = jnp.max(s, axis=0, keepdims=True)

    def consume(vt, s_ref, mt_ref):
        for i in range(2):
            m_old = m_ref[i]
            m_new = jnp.maximum(m_old, mt_ref[i])
            alpha = jnp.exp2(m_old - m_new)
            p = jnp.exp2(s_ref[i] - m_new)
            l_ref[i] = alpha * l_ref[i] + jnp.sum(p, axis=0, keepdims=True)
            acc_ref[i] = alpha * acc_ref[i] + _dot(vt, p.astype(MM))
            m_ref[i] = m_new

    @pl.when(j == 0)
    def _():
        qt = qt_ref[...]
        top = lax.broadcasted_iota(jnp.int32, qt.shape, 0) < HEAD
        zero = jnp.zeros_like(qt)
        qs_ref[0] = jnp.where(top, qt, zero)
        qs_ref[1] = jnp.where(top, zero, qt)
        m_ref[...] = jnp.full(m_ref.shape, NEG, F32)
        l_ref[...] = jnp.zeros(l_ref.shape, F32)
        acc_ref[...] = jnp.zeros(acc_ref.shape, F32)
        kc = kc_ref[...]
        for i in range(2):
            s = _dot(kc, qs_ref[i])
            m_new = jnp.max(s, axis=0, keepdims=True)
            p = jnp.exp2(s - m_new)
            l_ref[i] = jnp.sum(p, axis=0, keepdims=True)
            acc_ref[i] = _dot(vct_ref[...], p.astype(MM))
            m_ref[i] = m_new
        if stream:
            s1_ref[...] = jnp.full(s1_ref.shape, NEG, F32)
            mt1_ref[...] = jnp.full(mt1_ref.shape, NEG, F32)

    if stream:
        @pl.when(j % 2 == 0)
        def _():
            scores(k_ref[...], s0_ref, mt0_ref)
            consume(vt_ref[...], s1_ref, mt1_ref)

        @pl.when(j % 2 == 1)
        def _():
            scores(k_ref[...], s1_ref, mt1_ref)
            consume(vt_ref[...], s0_ref, mt0_ref)

    @pl.when(j == nj - 1)
    def _():
        lam4 = lam_ref[...]
        lam = (jnp.exp(jnp.sum(lam4[0:1] * lam4[1:2], axis=1, keepdims=True))
               - jnp.exp(jnp.sum(lam4[2:3] * lam4[3:4], axis=1, keepdims=True)) + lam_init)
        o_t = acc_ref[0] * (1.0 / l_ref[0]) - lam * (acc_ref[1] * (1.0 / l_ref[1]))
        o_ref[...] = _rms(o_t.T, sub_ref[...]) * (1.0 - lam_init)


def _diff_attention(k, qvt, lam4, subln, *, n_lat, lam_init, tq, tk):
    t, width = k.shape
    nc = t - n_lat
    nh = width // LANES
    ctx_blk = n_lat // nc

    def call(stream, q_rows, q_blk0, tq_, nk):
        in_specs = [pl.BlockSpec((LANES, tq_), lambda h, i, j: (h, q_blk0 + i)),
                    pl.BlockSpec((nc, LANES), lambda h, i, j: (ctx_blk, h)),
                    pl.BlockSpec((LANES, nc), lambda h, i, j: (nh + h, ctx_blk))]
        args = [qvt, k, qvt]
        scratch = [pltpu.VMEM((2, LANES, tq_), MM), pltpu.VMEM((2, 1, tq_), F32),
                   pltpu.VMEM((2, 1, tq_), F32), pltpu.VMEM((2, LANES, tq_), F32)]
        if stream:
            in_specs += [pl.BlockSpec((tk, LANES), lambda h, i, j: (jnp.minimum(j, nk - 1), h)),
                         pl.BlockSpec((LANES, tk), lambda h, i, j: (nh + h, jnp.maximum(j - 1, 0)))]
            args += [k, qvt]
            scratch += [pltpu.VMEM((2, tk, tq_), F32)] * 2 + [pltpu.VMEM((2, 1, tq_), F32)] * 2
        in_specs += [pl.BlockSpec((4, HEAD), lambda h, i, j: (0, 0)),
                     pl.BlockSpec((1, LANES), lambda h, i, j: (0, 0))]
        args += [lam4, subln]
        return pl.pallas_call(
            functools.partial(_diff_kernel, stream=stream, lam_init=lam_init),
            grid=(nh, q_rows // tq_, nk + 1 if stream else 1),
            in_specs=in_specs,
            out_specs=pl.BlockSpec((tq_, LANES), lambda h, i, j: (i, h)),
            out_shape=jax.ShapeDtypeStruct((q_rows, width), F32),
            scratch_shapes=scratch,
            compiler_params=_cparams("parallel", "parallel", "arbitrary"),
            name="diff_attn" if stream else "diff_attn_ctx",
        )(*args)

    assert n_lat % tq == 0 and n_lat % tk == 0 and n_lat % nc == 0
    o_lat = call(True, n_lat, 0, tq, n_lat // tk)
    o_ctx = call(False, nc, ctx_blk, nc, 0)
    return jnp.concatenate([o_lat, o_ctx], axis=0)


def _na_kernel(q_ref, kb_ref, vb_ref, kc_ref, vc_ref, bias_ref, o_ref):
    n_pairs = q_ref.shape[1] // LANES
    lo = _lane_lo((q_ref.shape[0], LANES))
    for p in range(n_pairs):
        sl = slice(p * LANES, (p + 1) * LANES)
        q = q_ref[:, sl]
        kb, vb, kc, vc = kb_ref[:, sl], vb_ref[:, sl], kc_ref[:, sl], vc_ref[:, sl]
        zero = jnp.zeros_like(q)
        outs = []
        for half in range(2):
            qh = jnp.where(lo, q, zero) if half == 0 else jnp.where(lo, zero, q)
            s_loc = _dot_nt(qh, kb) + bias_ref[2 * p + half]
            s_ctx = _dot_nt(qh, kc)
            m = jnp.maximum(jnp.max(s_loc, axis=1, keepdims=True), jnp.max(s_ctx, axis=1, keepdims=True))
            p_loc = jnp.exp(s_loc - m)
            p_ctx = jnp.exp(s_ctx - m)
            l = jnp.sum(p_loc, axis=1, keepdims=True) + jnp.sum(p_ctx, axis=1, keepdims=True)
            outs.append((_dot(p_loc.astype(MM), vb) + _dot(p_ctx.astype(MM), vc)) / l)
        o_ref[:, sl] = jnp.where(lo, outs[0], outs[1])


def _na_ctx_kernel(q_ref, k_ref, v_ref, o_ref):
    q, k, v = q_ref[...], k_ref[...], v_ref[...]
    lo = _lane_lo(q.shape)
    zero = jnp.zeros_like(q)
    outs = []
    for half in range(2):
        qh = jnp.where(lo, q, zero) if half == 0 else jnp.where(lo, zero, q)
        s = _dot_nt(qh, k)
        p = jnp.exp(s - jnp.max(s, axis=1, keepdims=True))
        outs.append(_dot(p.astype(MM), v) / jnp.sum(p, axis=1, keepdims=True))
    o_ref[...] = jnp.where(lo, outs[0], outs[1])


def _na_bias_table(rpb, rows):
    kr = min(NA_KR, rows)
    kw = min(NA_KC, GRID_W)
    w = np.arange(GRID_W)
    c0 = np.clip(w - kw // 2, 0, GRID_W - kw)
    kc = np.arange(GRID_W)
    valid = (kc[None, :] >= c0[:, None]) & (kc[None, :] < c0[:, None] + kw)
    col = np.clip(kc[None, :] - w[:, None] + (NA_KC - 1), 0, 2 * NA_KC - 2)
    row = np.arange(NA_KR)[:, None] + np.arange(kr)[None, :]
    tab = rpb[:, row][:, :, :, col]
    tab = jnp.where(valid[None, None, None], tab, NEG)
    tab = tab.transpose(1, 0, 3, 2, 4)
    return tab.reshape(NA_KR, rpb.shape[0], GRID_W, kr * GRID_W).astype(F32)


def _na_attention(qkv, rpb, *, n_lat):
    t = qkv.shape[0]
    nc = t - n_lat
    width = qkv.shape[1] // 3
    rows = n_lat // GRID_W
    kr = min(NA_KR, rows)
    band = kr * GRID_W
    ctx_blk = n_lat // nc
    table = _na_bias_table(rpb, rows)

    def r0(r):
        return jnp.clip(r - kr // 2, 0, rows - kr)

    o_lat = pl.pallas_call(
        _na_kernel,
        grid=(rows,),
        in_specs=[pl.BlockSpec((GRID_W, width), lambda r: (r, 0)),
                  pl.BlockSpec((pl.Element(band), pl.Element(width)), lambda r: (r0(r) * GRID_W, width)),
                  pl.BlockSpec((pl.Element(band), pl.Element(width)), lambda r: (r0(r) * GRID_W, 2 * width)),
                  pl.BlockSpec((nc, width), lambda r: (ctx_blk, 1)),
                  pl.BlockSpec((nc, width), lambda r: (ctx_blk, 2)),
                  pl.BlockSpec((None, rpb.shape[0], GRID_W, band), lambda r: (r0(r) - r + NA_KR - 1, 0, 0, 0))],
        out_specs=pl.BlockSpec((GRID_W, width), lambda r: (r, 0)),
        out_shape=jax.ShapeDtypeStruct((n_lat, width), F32),
        compiler_params=_cparams("parallel"),
        name="na_attn",
    )(qkv, qkv, qkv, qkv, qkv, table)

    nh = width // LANES
    o_ctx = pl.pallas_call(
        _na_ctx_kernel,
        grid=(nh,),
        in_specs=[pl.BlockSpec((nc, LANES), lambda h: (ctx_blk, h)),
                  pl.BlockSpec((nc, LANES), lambda h: (ctx_blk, nh + h)),
                  pl.BlockSpec((nc, LANES), lambda h: (ctx_blk, 2 * nh + h))],
        out_specs=pl.BlockSpec((nc, LANES), lambda h: (0, h)),
        out_shape=jax.ShapeDtypeStruct((nc, width), F32),
        compiler_params=_cparams("parallel"),
        name="na_attn_ctx",
    )(qkv, qkv, qkv)
    return jnp.concatenate([o_lat, o_ctx], axis=0)


def _rope_tables(n_lat, n_ctx):
    tok = jnp.arange(n_lat)
    row = (tok // GRID_W).astype(F32)
    col = (tok % GRID_W).astype(F32)
    n_freq = HEAD // 4
    inv = ROPE_BASE ** (-jnp.arange(n_freq, dtype=F32) / n_freq)
    ang = jnp.concatenate([row[:, None] * inv, col[:, None] * inv], axis=-1)
    cos, sin = jnp.cos(ang), jnp.sin(ang)
    cos = jnp.concatenate([jnp.tile(cos, (1, 4)), jnp.ones((n_ctx, LANES), F32)], axis=0)
    sin = jnp.concatenate([jnp.concatenate([-sin, sin, -sin, sin], axis=1), jnp.zeros((n_ctx, LANES), F32)], axis=0)
    return cos, sin


def _pick_tm(t, cap):
    return max(m for m in range(8, cap + 1, 8) if t % m == 0)


def kernel(x, c, ctx, c_ctx, ada_w, ada_b, norm_pre, norm_post, ssd_w_in, ssd_conv_w, ssd_conv_b, ssd_dt_bias, ssd_a_log, ssd_d, ssd_norm, ssd_w_out, dif_w_in, dif_lam_q1, dif_lam_k1, dif_lam_q2, dif_lam_k2, dif_subln, dif_w_out, na_w_in, na_rpb, na_w_out):
    n_lat, d = x.shape[1], x.shape[2]
    n_ctx = ctx.shape[1]
    depth = ada_w.shape[0]
    xa = jnp.concatenate([x[0], ctx[0]], axis=0)
    t = xa.shape[0]
    mods = _modulation(c, c_ctx, ada_w, ada_b)
    rope = None
    tm_in = _pick_tm(t, 1280)
    tm_out = _pick_tm(t, 640)
    ia = ib = ic = 0
    for i in range(depth):
        shift, scale, gate = (mods[i, 0:2, k * d:(k + 1) * d] for k in range(3))
        nw = norm_pre[i][None, :]
        pw = norm_post[i][None, :]
        inproj = functools.partial(_inproj, xa, nw, scale, shift, n_lat=n_lat, tm=tm_in)
        kind = i % 3
        if kind == 0:
            w_in = ssd_w_in[ia]
            inner = ssd_w_out.shape[1]
            gn2 = w_in.shape[1] - 2 * inner - 2 * SSD_HEADS
            u = inproj(w_in[:, :2 * inner + gn2].astype(MM), out_dtype=F32, tn=512, name="ssd_inproj")
            w_dt = jnp.zeros((d, 2 * LANES), F32)
            w_dt = w_dt.at[:, :SSD_HEADS].set(w_in[:, 2 * inner + gn2:2 * inner + gn2 + SSD_HEADS])
            w_dt = w_dt.at[:, LANES:LANES + SSD_HEADS].set(w_in[:, 2 * inner + gn2 + SSD_HEADS:])
            dtp = inproj(w_dt.astype(MM), out_dtype=F32, tn=2 * LANES, name="ssd_inproj_dt")
            cw, cb = ssd_conv_w[ia], ssd_conv_b[ia][None, :]
            xs = _conv_silu(u, inner, inner, cw[:, :inner], cb[:, :inner], n_lat=n_lat, tm=n_ctx, tc=512,
                            out_dtype=F32, name="ssd_conv_x")
            bc = _conv_silu(u, 2 * inner, gn2, cw[:, inner:], cb[:, inner:], n_lat=n_lat, tm=n_ctx, tc=512,
                            out_dtype=MM, name="ssd_conv_bc")
            y = _ssd_scan(xs, bc, dtp, ssd_dt_bias[ia], ssd_a_log[ia], ssd_d[ia], n_lat=n_lat)
            z = u[:, :inner]
            xa = _outproj([y[0], y[1]], z, ssd_norm[ia][None, :], ssd_w_out[ia].astype(MM), xa, pw, gate,
                          n_lat=n_lat, tm=tm_out, name="ssd_outproj")
            ia += 1
        elif kind == 1:
            lam_init = 0.8 - 0.6 * math.exp(-0.3 * i)
            w_in = dif_w_in[ib]
            width = dif_w_out.shape[1]
            if rope is None:
                rope = _rope_tables(n_lat, n_ctx)
            rope_t = tuple(tab.T for tab in rope)
            k = inproj(w_in[:, width:2 * width].astype(MM), out_dtype=MM, tn=256, n_rope=width // 256,
                       rope=rope, name="dif_inproj_k")
            w_qv = jnp.concatenate([w_in[:, :width], w_in[:, 2 * width:3 * width]], axis=1).T.astype(MM)
            qvt = inproj(w_qv, out_dtype=MM, tn=256, n_q=width // 256, n_rope=width // 256,
                         q_scale=HEAD ** -0.5 * math.log2(math.e), rope=rope_t, transposed=True,
                         name="dif_inproj_qvt")
            g = inproj(w_in[:, 3 * width:].astype(MM), out_dtype=F32, tn=512, name="dif_inproj_g")
            lam4 = jnp.stack([dif_lam_q1[ib], dif_lam_k1[ib], dif_lam_q2[ib], dif_lam_k2[ib]])
            o = _diff_attention(k, qvt, lam4, dif_subln[ib][None, :], n_lat=n_lat, lam_init=lam_init,
                                tq=min(512, n_lat), tk=min(1024, n_lat))
            xa = _outproj([o], g, None, dif_w_out[ib].astype(MM), xa, pw, gate, n_lat=n_lat, tm=tm_out,
                          name="dif_outproj")
            ib += 1
        else:
            w_in = na_w_in[ic]
            qkv = inproj(w_in[:, :3 * d].astype(MM), out_dtype=MM, tn=256, n_q=d // 256,
                         q_scale=HEAD ** -0.5, name="na_inproj_qkv")
            g = inproj(w_in[:, 3 * d:].astype(MM), out_dtype=F32, tn=512, name="na_inproj_g")
            o = _na_attention(qkv, na_rpb[ic], n_lat=n_lat)
            xa = _outproj([o], g, None, na_w_out[ic].astype(MM), xa, pw, gate, n_lat=n_lat, tm=tm_out,
                          name="na_outproj")
            ic += 1
    return xa[:n_lat][None]
```

```python
import functools
import math

import numpy as np
import jax
import jax.numpy as jnp
from jax import lax
from jax.experimental import pallas as pl
from jax.experimental.pallas import tpu as pltpu

F32 = jnp.float32
MM = jnp.bfloat16

EPS = 1e-6
GRID_W = 64
LANES = 128
HEAD = 64
NEG = -1e30
LOG2E = math.log2(math.e)

SSD_HEADS = 32
SSD_STATE = 128
SSD_CONV = 5
SSD_CHUNK = 128

NA_KR = 8
NA_KC = 16
NA_RB = 4
NA_WIN = NA_RB + NA_KR
ROPE_BASE = 10000.0

VMEM_LIMIT = 52 * 1024 * 1024


def _cparams(*sem):
    return pltpu.CompilerParams(dimension_semantics=sem, vmem_limit_bytes=VMEM_LIMIT)


def _silu(v):
    return v * (1.0 / (1.0 + jnp.exp(-v)))


def _softplus(v):
    return jnp.maximum(v, 0.0) + jnp.log(1.0 + jnp.exp(-jnp.abs(v)))


def _rms(v, w):
    return v * lax.rsqrt(jnp.mean(v * v, axis=-1, keepdims=True) + EPS) * w


def _dot(a, b):
    return jnp.dot(a, b, preferred_element_type=F32)


def _dot_nt(a, b):
    return lax.dot_general(a, b, (((1,), (1,)), ((), ())), preferred_element_type=F32)


def _lane_lo(shape):
    return lax.broadcasted_iota(jnp.int32, shape, len(shape) - 1) % LANES < HEAD


def _split_heads(qt):
    top = lax.broadcasted_iota(jnp.int32, qt.shape, 0) < HEAD
    zero = jnp.zeros_like(qt)
    return jnp.where(top, qt, zero), jnp.where(top, zero, qt)


def _mod_kernel(cc_ref, w_ref, b_ref, o_ref):
    s = _silu(cc_ref[...])
    o_ref[0] = _dot(s.astype(MM), w_ref[0].astype(MM)) + b_ref[0]


def _modulation(c, c_ctx, ada_w, ada_b):
    depth, d, d3 = ada_w.shape
    cc = jnp.zeros((8, d), F32).at[0].set(c[0]).at[1].set(c_ctx)
    out = pl.pallas_call(
        _mod_kernel,
        grid=(depth, d3 // d),
        in_specs=[pl.BlockSpec((8, d), lambda i, j: (0, 0)),
                  pl.BlockSpec((1, d, d), lambda i, j: (i, 0, j)),
                  pl.BlockSpec((1, 1, d), lambda i, j: (i, 0, j))],
        out_specs=pl.BlockSpec((1, 8, d), lambda i, j: (i, 0, j)),
        out_shape=jax.ShapeDtypeStruct((depth, 8, d3), F32),
        compiler_params=_cparams("parallel", "parallel"),
        name="adaln_mod",
    )(cc, ada_w, ada_b.reshape(depth, 1, d3))
    return out


def _row_select(i, tm, n_lat, v2):
    rows = i * tm + lax.broadcasted_iota(jnp.int32, (tm, 1), 0)
    return jnp.where(rows < n_lat, v2[0:1, :], v2[1:2, :])


def _inproj_kernel(*refs, tm, n_lat, n_q, n_rope, q_scale, transposed):
    if n_rope:
        x_ref, nw_ref, sc_ref, sh_ref, w_ref, cos_ref, sin_ref, o_ref, h_ref = refs
    else:
        x_ref, nw_ref, sc_ref, sh_ref, w_ref, o_ref, h_ref = refs
    i, j = pl.program_id(0), pl.program_id(1)
    fa = 0 if transposed else 1

    @pl.when(j == 0)
    def _():
        h = _rms(x_ref[...], nw_ref[...])
        h = h * (1.0 + _row_select(i, tm, n_lat, sc_ref[...])) + _row_select(i, tm, n_lat, sh_ref[...])
        h_ref[...] = h.astype(MM)

    acc = _dot_nt(w_ref[...], h_ref[...]) if transposed else _dot(h_ref[...], w_ref[...])
    if n_q:
        acc = acc * jnp.where(j < n_q, q_scale, 1.0)
    if n_rope:
        tn = acc.shape[fa]

        @pl.when(j < n_rope)
        def _():
            reps = tn // LANES
            cos = jnp.concatenate([cos_ref[...]] * reps, axis=fa)
            sin = jnp.concatenate([sin_ref[...]] * reps, axis=fa)
            feat = lax.broadcasted_iota(jnp.int32, acc.shape, fa)
            swapped = jnp.where(feat % HEAD < HEAD // 2,
                                pltpu.roll(acc, tn - HEAD // 2, fa), pltpu.roll(acc, HEAD // 2, fa))
            o_ref[...] = (acc * cos + swapped * sin).astype(o_ref.dtype)

        @pl.when(j >= n_rope)
        def _():
            o_ref[...] = acc.astype(o_ref.dtype)
    else:
        o_ref[...] = acc.astype(o_ref.dtype)


def _inproj(xa, nw, sc, sh, w, *, n_lat, out_dtype, tm, tn, n_q=0, n_rope=0, q_scale=1.0, rope=None,
            transposed=False, name="inproj"):
    t, d = xa.shape
    f = w.shape[0] if transposed else w.shape[1]
    assert t % tm == 0 and f % tn == 0
    in_specs = [pl.BlockSpec((tm, d), lambda i, j: (i, 0)),
                pl.BlockSpec((1, d), lambda i, j: (0, 0)),
                pl.BlockSpec((2, d), lambda i, j: (0, 0)),
                pl.BlockSpec((2, d), lambda i, j: (0, 0)),
                pl.BlockSpec((tn, d), lambda i, j: (j, 0)) if transposed else pl.BlockSpec((d, tn), lambda i, j: (0, j))]
    args = [xa, nw, sc, sh, w]
    if n_rope:
        tab = pl.BlockSpec((LANES, tm), lambda i, j: (0, i)) if transposed else pl.BlockSpec((tm, LANES), lambda i, j: (i, 0))
        in_specs += [tab, tab]
        args += list(rope)
    if transposed:
        out_spec, out_shape = pl.BlockSpec((tn, tm), lambda i, j: (j, i)), (f, t)
    else:
        out_spec, out_shape = pl.BlockSpec((tm, tn), lambda i, j: (i, j)), (t, f)
    return pl.pallas_call(
        functools.partial(_inproj_kernel, tm=tm, n_lat=n_lat, n_q=n_q, n_rope=n_rope, q_scale=q_scale,
                          transposed=transposed),
        grid=(t // tm, f // tn),
        in_specs=in_specs,
        out_specs=out_spec,
        out_shape=jax.ShapeDtypeStruct(out_shape, out_dtype),
        scratch_shapes=[pltpu.VMEM((tm, d), MM)],
        compiler_params=_cparams("parallel", "arbitrary"),
        name=name,
    )(*args)


def _outproj_kernel(*refs, tm, n_lat, n_a, combine, use_norm):
    a_refs = refs[:n_a]
    g_ref = refs[n_a]
    rest = refs[n_a + 1:]
    if use_norm:
        nm_ref, rest = rest[0], rest[1:]
    w_ref, x_ref, pw_ref, gate_ref, o_ref = rest
    i = pl.program_id(0)
    if combine == "sum":
        a = a_refs[0][...]
        for r in a_refs[1:]:
            a = a + r[...]
    else:
        a = jnp.where(i * tm < n_lat, a_refs[0][...], a_refs[1][...])
    a = a * _silu(g_ref[...])
    if use_norm:
        a = _rms(a, nm_ref[...])
    o = _dot(a.astype(MM), w_ref[...])
    o = _rms(o, pw_ref[...])
    o_ref[...] = x_ref[...] + _row_select(i, tm, n_lat, gate_ref[...]) * o


def _outproj(a_ops, combine, g_op, norm_w, w, xa, pw, gate, *, n_lat, tm, out_rows, name="outproj"):
    t, d = xa.shape
    f = w.shape[0]
    assert out_rows % tm == 0 and (n_lat % tm == 0 or combine == "sum")
    tok = pl.BlockSpec((tm, d), lambda i: (i, 0))
    full = lambda r, c: pl.BlockSpec((r, c), lambda i: (0, 0))
    in_specs = [spec for _, spec in a_ops] + [g_op[1]]
    args = [arr for arr, _ in a_ops] + [g_op[0]]
    if norm_w is not None:
        in_specs.append(full(1, f))
        args.append(norm_w)
    in_specs += [full(f, d), tok, full(1, d), full(2, d)]
    args += [w, xa, pw, gate]
    return pl.pallas_call(
        functools.partial(_outproj_kernel, tm=tm, n_lat=n_lat, n_a=len(a_ops), combine=combine,
                          use_norm=norm_w is not None),
        grid=(out_rows // tm,),
        in_specs=in_specs,
        out_specs=tok,
        out_shape=jax.ShapeDtypeStruct((out_rows, d), F32),
        compiler_params=_cparams("parallel"),
        name=name,
    )(*args)


def _attn_out_ops(o_lat, o_ctx, tm):
    nl = o_lat.shape[0] // tm
    f = o_lat.shape[1]
    return [(o_lat, pl.BlockSpec((tm, f), lambda i: (jnp.minimum(i, nl - 1), 0))),
            (o_ctx, pl.BlockSpec((tm, f), lambda i: (jnp.maximum(i - nl, 0), 0)))]


def _conv_kernel(prev_ref, cur_ref, next_ref, w_ref, b_ref, o_ref, *, tm, first_blocks, last_blocks):
    i = pl.program_id(0)
    is_first = functools.reduce(jnp.logical_or, [i == b for b in first_blocks])
    is_last = functools.reduce(jnp.logical_or, [i == b for b in last_blocks])
    u = jnp.concatenate([jnp.where(is_first, 0.0, prev_ref[...]), cur_ref[...],
                         jnp.where(is_last, 0.0, next_ref[...])], axis=0)
    rows = u.shape[0]
    tap = [w_ref[k:k + 1, :] * u for k in range(SSD_CONV)]
    later = pltpu.roll(tap[3] + pltpu.roll(tap[4], rows - 1, 0), rows - 1, 0)
    earlier = pltpu.roll(tap[1] + pltpu.roll(tap[0], 1, 0), 1, 0)
    acc = (tap[2] + later + earlier)[8:8 + tm] + b_ref[...]
    o_ref[...] = _silu(acc).astype(o_ref.dtype)


def _conv_silu(u, col0, width, conv_w, conv_b, *, n_lat, tm, tc, out_dtype, name):
    assert SSD_CONV == 5
    t = u.shape[0]
    assert t % tm == 0 and n_lat % tm == 0 and width % tc == 0 and col0 % tc == 0 and tm % 8 == 0
    cb0 = col0 // tc
    r8 = tm // 8
    nb8 = t // 8
    nl, nt = n_lat // tm, t // tm
    return pl.pallas_call(
        functools.partial(_conv_kernel, tm=tm, first_blocks=(0, nl), last_blocks=(nl - 1, nt - 1)),
        grid=(nt, width // tc),
        in_specs=[pl.BlockSpec((8, tc), lambda i, j: (jnp.maximum(i * r8 - 1, 0), cb0 + j)),
                  pl.BlockSpec((tm, tc), lambda i, j: (i, cb0 + j)),
                  pl.BlockSpec((8, tc), lambda i, j: (jnp.minimum((i + 1) * r8, nb8 - 1), cb0 + j)),
                  pl.BlockSpec((SSD_CONV, tc), lambda i, j: (0, j)),
                  pl.BlockSpec((1, tc), lambda i, j: (0, j))],
        out_specs=pl.BlockSpec((tm, tc), lambda i, j: (i, j)),
        out_shape=jax.ShapeDtypeStruct((t, width), out_dtype),
        compiler_params=_cparams("parallel", "parallel"),
        name=name,
    )(u, u, u, conv_w, conv_b)


def _ssd_kernel(x_ref, b_ref, c_ref, dt_ref, bias_ref, alog_ref, dskip_ref, y_ref, h_ref):
    d, s = pl.program_id(0), pl.program_id(1)
    q = SSD_CHUNK

    @pl.when(s == 0)
    def _():
        h_ref[...] = jnp.zeros(h_ref.shape, F32)

    dt = _softplus(dt_ref[...] + bias_ref[...])
    a = dt * (-jnp.exp(alog_ref[...]))
    ii = lax.broadcasted_iota(jnp.int32, (q, q), 0)
    jj = lax.broadcasted_iota(jnp.int32, (q, q), 1)
    keep = jnp.where(d == 0, jj - ii, ii - jj) <= 0
    cs = jnp.dot(keep.astype(F32), a, preferred_element_type=F32, precision=lax.Precision.HIGHEST)
    cs_t = cs.T
    dt_t = dt.T
    tot = jnp.sum(a, axis=0, keepdims=True)
    tot_t = jnp.sum(a.T, axis=1, keepdims=True)
    w_t = jnp.exp(tot_t - cs_t) * dt_t
    dec = jnp.exp(tot)
    lane_lo = _lane_lo((q, LANES))
    skip_on = jnp.where(d == 0, 1.0, 0.0)

    n_groups = b_ref.shape[1] // SSD_STATE
    hpg = SSD_HEADS // n_groups
    for g in range(n_groups):
        bg = b_ref[:, g * SSD_STATE:(g + 1) * SSD_STATE]
        cg = c_ref[:, g * SSD_STATE:(g + 1) * SSD_STATE]
        cb = _dot_nt(cg, bg)
        bg_t = bg.astype(F32).T
        cg32 = cg.astype(F32)
        for pr in range(hpg // 2):
            p = g * (hpg // 2) + pr
            x32 = x_ref[:, p * LANES:(p + 1) * LANES]
            xb = x32.astype(MM)
            hp = h_ref[p]
            lhs_y, lhs_h = [], []
            for half in range(2):
                hd = 2 * p + half
                col = jnp.broadcast_to(cs[:, hd:hd + 1], (q, q))
                row = jnp.broadcast_to(cs_t[hd:hd + 1, :], (q, q))
                lmat = jnp.exp(jnp.where(keep, col - row, NEG))
                m = cb * lmat * dt_t[hd:hd + 1, :]
                ce = cg32 * jnp.exp(col)
                lhs_y.append(jnp.concatenate([m.astype(MM), ce.astype(MM)], axis=1))
                lhs_h.append((bg_t * w_t[hd:hd + 1, :]).astype(MM))
            rhs = jnp.concatenate([xb, hp.astype(MM)], axis=0)
            yf = _dot(jnp.concatenate(lhs_y, axis=0), rhs)
            hf = _dot(jnp.concatenate(lhs_h, axis=0), xb)
            y = jnp.where(lane_lo, yf[:q], yf[q:])
            y_ref[0, :, p * LANES:(p + 1) * LANES] = y + skip_on * dskip_ref[:, p * LANES:(p + 1) * LANES] * x32
            hn = jnp.where(lane_lo, hf[:SSD_STATE], hf[SSD_STATE:])
            hd0 = 2 * p
            dec_p = jnp.where(lane_lo[0:1, :], dec[:, hd0:hd0 + 1], dec[:, hd0 + 1:hd0 + 2])
            h_ref[p] = hp * dec_p + hn


def _ssd_scan(xs, bc, dtp, dt_bias, a_log, d_skip, *, n_lat):
    t, inner = xs.shape
    gn = bc.shape[1] // 2
    q = SSD_CHUNK
    nlc, ntc = n_lat // q, t // q

    def chunk(d, s):
        fwd = jnp.where(s < ntc - nlc, nlc + s, s - (ntc - nlc))
        return jnp.where(d == 0, fwd, ntc - 1 - s)

    pad = lambda v: jnp.zeros((2, 1, LANES), F32).at[:, 0, :SSD_HEADS].set(v)
    dsk = jnp.repeat(d_skip, inner // SSD_HEADS)[None, :]
    return pl.pallas_call(
        _ssd_kernel,
        grid=(2, ntc),
        in_specs=[pl.BlockSpec((q, inner), lambda d, s: (chunk(d, s), 0)),
                  pl.BlockSpec((q, gn), lambda d, s: (chunk(d, s), 0)),
                  pl.BlockSpec((q, gn), lambda d, s: (chunk(d, s), 1)),
                  pl.BlockSpec((q, LANES), lambda d, s: (chunk(d, s), d)),
                  pl.BlockSpec((None, 1, LANES), lambda d, s: (d, 0, 0)),
                  pl.BlockSpec((None, 1, LANES), lambda d, s: (d, 0, 0)),
                  pl.BlockSpec((1, inner), lambda d, s: (0, 0))],
        out_specs=pl.BlockSpec((1, q, inner), lambda d, s: (d, chunk(d, s), 0)),
        out_shape=jax.ShapeDtypeStruct((2, t, inner), F32),
        scratch_shapes=[pltpu.VMEM((SSD_HEADS // 2, SSD_STATE, LANES), F32)],
        compiler_params=_cparams("parallel", "arbitrary"),
        name="ssd_scan",
    )(xs, bc, bc, dtp, pad(dt_bias), pad(a_log), dsk)


def _diff_kernel(*refs, stream, lam_init):
    if stream:
        (qt_ref, kc_ref, vct_ref, k_ref, vt_ref, lam_ref, sub_ref, o_ref,
         qs_ref, m_ref, l_ref, acc_ref, s0_ref, s1_ref, mt0_ref, mt1_ref) = refs
    else:
        qt_ref, kc_ref, vct_ref, lam_ref, sub_ref, o_ref, qs_ref, m_ref, l_ref, acc_ref = refs
    j = pl.program_id(2)
    nj = pl.num_programs(2)

    tq = qt_ref.shape[1]
    qc = min(tq, 512)
    chunks = [(i, slice(c, c + qc)) for i in range(2) for c in range(0, tq, qc)]

    def scores(k, s_ref, mt_ref):
        for i, cs in chunks:
            s = _dot(k, qs_ref[i, :, cs])
            s_ref[i, :, cs] = s
            mt_ref[i, :, cs] = jnp.max(s, axis=0, keepdims=True)

    def consume(vt, s_ref, mt_ref):
        for i, cs in chunks:
            m_old = m_ref[i, :, cs]
            m_new = jnp.maximum(m_old, mt_ref[i, :, cs])
            alpha = jnp.exp2(m_old - m_new)
            p = jnp.exp2(s_ref[i, :, cs] - m_new)
            l_ref[i, :, cs] = alpha * l_ref[i, :, cs] + jnp.sum(p, axis=0, keepdims=True)
            acc_ref[i, :, cs] = alpha * acc_ref[i, :, cs] + _dot(vt, p.astype(MM))
            m_ref[i, :, cs] = m_new

    @pl.when(j == 0)
    def _():
        qs_ref[0], qs_ref[1] = _split_heads(qt_ref[...])
        kc = kc_ref[...]
        for i in range(2):
            s = _dot(kc, qs_ref[i])
            m_new = jnp.max(s, axis=0, keepdims=True)
            p = jnp.exp2(s - m_new)
            l_ref[i] = jnp.sum(p, axis=0, keepdims=True)
            acc_ref[i] = _dot(vct_ref[...], p.astype(MM))
            m_ref[i] = m_new
        if stream:
            s1_ref[...] = jnp.full(s1_ref.shape, NEG, F32)
            mt1_ref[...] = jnp.full(mt1_ref.shape, NEG, F32)

    if stream:
        @pl.when(j % 2 == 0)
        def _():
            scores(k_ref[...], s0_ref, mt0_ref)
            consume(vt_ref[...], s1_ref, mt1_ref)

        @pl.when(j % 2 == 1)
        def _():
            scores(k_ref[...], s1_ref, mt1_ref)
            consume(vt_ref[...], s0_ref, mt0_ref)

    @pl.when(j == nj - 1)
    def _():
        lam4 = lam_ref[...]
        lam = (jnp.exp(jnp.sum(lam4[0:1] * lam4[1:2], axis=1, keepdims=True))
               - jnp.exp(jnp.sum(lam4[2:3] * lam4[3:4], axis=1, keepdims=True)) + lam_init)
        o_t = acc_ref[0] * (1.0 / l_ref[0]) - lam * (acc_ref[1] * (1.0 / l_ref[1]))
        o_ref[...] = _rms(o_t.T, sub_ref[...]) * (1.0 - lam_init)


def _diff_attention(k, qvt, lam4, subln, *, n_lat, lam_init, tq, tk):
    t, width = k.shape
    nc = t - n_lat
    nh = width // LANES
    ctx_blk = n_lat // nc

    def call(stream, q_rows, q_blk0, tq_, nk):
        in_specs = [pl.BlockSpec((LANES, tq_), lambda h, i, j: (h, q_blk0 + i)),
                    pl.BlockSpec((nc, LANES), lambda h, i, j: (ctx_blk, h)),
                    pl.BlockSpec((LANES, nc), lambda h, i, j: (nh + h, ctx_blk))]
        args = [qvt, k, qvt]
        scratch = [pltpu.VMEM((2, LANES, tq_), MM), pltpu.VMEM((2, 1, tq_), F32),
                   pltpu.VMEM((2, 1, tq_), F32), pltpu.VMEM((2, LANES, tq_), F32)]
        if stream:
            in_specs += [pl.BlockSpec((tk, LANES), lambda h, i, j: (jnp.minimum(j, nk - 1), h)),
                         pl.BlockSpec((LANES, tk), lambda h, i, j: (nh + h, jnp.maximum(j - 1, 0)))]
            args += [k, qvt]
            scratch += [pltpu.VMEM((2, tk, tq_), F32)] * 2 + [pltpu.VMEM((2, 1, tq_), F32)] * 2
        in_specs += [pl.BlockSpec((4, HEAD), lambda h, i, j: (0, 0)),
                     pl.BlockSpec((1, LANES), lambda h, i, j: (0, 0))]
        args += [lam4, subln]
        return pl.pallas_call(
            functools.partial(_diff_kernel, stream=stream, lam_init=lam_init),
            grid=(nh, q_rows // tq_, nk + 1 if stream else 1),
            in_specs=in_specs,
            out_specs=pl.BlockSpec((tq_, LANES), lambda h, i, j: (i, h)),
            out_shape=jax.ShapeDtypeStruct((q_rows, width), F32),
            scratch_shapes=scratch,
            compiler_params=_cparams("parallel", "parallel", "arbitrary"),
            name="diff_attn" if stream else "diff_attn_ctx",
        )(*args)

    assert n_lat % tq == 0 and n_lat % tk == 0 and n_lat % nc == 0
    return call(True, n_lat, 0, tq, n_lat // tk), call(False, nc, ctx_blk, nc, 0)


def _na_kernel(*refs, local):
    if local:
        qt_ref, kw_ref, vwt_ref, kc_ref, vct_ref, bias_ref, o_ref = refs
    else:
        qt_ref, kc_ref, vct_ref, o_ref = refs
    outs = []
    for half, qh in enumerate(_split_heads(qt_ref[...])):
        s_c = _dot(kc_ref[...], qh)
        m = jnp.max(s_c, axis=0, keepdims=True)
        if local:
            s_l = _dot(kw_ref[...], qh) + bias_ref[half]
            m = jnp.maximum(m, jnp.max(s_l, axis=0, keepdims=True))
        p_c = jnp.exp2(s_c - m)
        l = jnp.sum(p_c, axis=0, keepdims=True)
        o_t = _dot(vct_ref[...], p_c.astype(MM))
        if local:
            p_l = jnp.exp2(s_l - m)
            l = l + jnp.sum(p_l, axis=0, keepdims=True)
            o_t = o_t + _dot(vwt_ref[...], p_l.astype(MM))
        outs.append(o_t * (1.0 / l))
    top = lax.broadcasted_iota(jnp.int32, outs[0].shape, 0) < HEAD
    o_ref[...] = jnp.where(top, outs[0], outs[1]).T


def _na_geometry(rows):
    kr = min(NA_KR, rows)
    assert kr == NA_KR and rows % NA_RB == 0 and rows >= NA_WIN and (rows - NA_WIN) % 2 == 0
    nb = rows // NA_RB
    ws = np.clip(NA_RB * np.arange(nb) - kr // 2, 0, rows - NA_WIN)

    def rowpart(b):
        r = NA_RB * b + np.arange(NA_RB)
        r0 = np.clip(r - kr // 2, 0, rows - kr)
        key = ws[b] + np.arange(NA_WIN)
        valid = (key[:, None] >= r0[None, :]) & (key[:, None] < r0[None, :] + kr)
        brow = np.clip(key[:, None] - r[None, :] + (NA_KR - 1), 0, 2 * NA_KR - 2)
        return valid, brow

    variant_blocks = [0, min(1, nb - 1), nb - 1]
    variant_of = np.where(np.arange(nb) == 0, 0, np.where(np.arange(nb) == nb - 1, 2, 1))
    for b in range(nb):
        vb, bb = rowpart(b)
        vv, bv = rowpart(variant_blocks[variant_of[b]])
        assert (vb == vv).all() and (np.where(vb, bb, 0) == np.where(vv, bv, 0)).all()
    return nb, ws, [rowpart(b) for b in variant_blocks]


def _na_bias_table(rpb, rows):
    _, _, parts = _na_geometry(rows)
    kw = min(NA_KC, GRID_W)
    w = np.arange(GRID_W)
    c0 = np.clip(w - kw // 2, 0, GRID_W - kw)
    kc = np.arange(GRID_W)
    cvalid = (kc[:, None] >= c0[None, :]) & (kc[:, None] < c0[None, :] + kw)
    bcol = np.clip(kc[:, None] - w[None, :] + (NA_KC - 1), 0, 2 * NA_KC - 2)
    tabs = []
    for rvalid, brow in parts:
        tab = rpb[:, brow[:, None, :, None], bcol[None, :, None, :]]
        valid = rvalid[:, None, :, None] & cvalid[None, :, None, :]
        tab = jnp.where(valid[None], tab * LOG2E, NEG)
        tabs.append(tab.reshape(rpb.shape[0], NA_WIN * GRID_W, NA_RB * GRID_W))
    return jnp.stack(tabs).astype(F32)


def _na_attention(k, qvt, rpb, *, n_lat):
    t, width = k.shape
    nc = t - n_lat
    nh = width // LANES
    rows = n_lat // GRID_W
    ctx_blk = n_lat // nc
    nb, ws_np, _ = _na_geometry(rows)
    table = _na_bias_table(rpb, rows)
    nq, win = NA_RB * GRID_W, NA_WIN * GRID_W
    assert nc % nq == 0

    def ws(b):
        return jnp.clip(NA_RB // 2 * b - NA_KR // 4, 0, (rows - NA_WIN) // 2) * (2 * GRID_W)

    def variant(b):
        return jnp.where(b == 0, 0, jnp.where(b == nb - 1, 2, 1))

    assert (ws_np == np.clip(NA_RB * np.arange(nb) - NA_KR // 2, 0, rows - NA_WIN)).all()
    o_lat = pl.pallas_call(
        functools.partial(_na_kernel, local=True),
        grid=(nh, nb),
        in_specs=[pl.BlockSpec((LANES, nq), lambda p, b: (p, b)),
                  pl.BlockSpec((pl.Element(win), pl.Element(LANES)), lambda p, b: (ws(b), p * LANES)),
                  pl.BlockSpec((pl.Element(LANES), pl.Element(win)), lambda p, b: ((nh + p) * LANES, ws(b))),
                  pl.BlockSpec((nc, LANES), lambda p, b: (ctx_blk, p)),
                  pl.BlockSpec((LANES, nc), lambda p, b: (nh + p, ctx_blk)),
                  pl.BlockSpec((None, 2, win, nq), lambda p, b: (variant(b), p, 0, 0))],
        out_specs=pl.BlockSpec((nq, LANES), lambda p, b: (b, p)),
        out_shape=jax.ShapeDtypeStruct((n_lat, width), F32),
        compiler_params=_cparams("parallel", "parallel"),
        name="na_attn",
    )(qvt, k, qvt, k, qvt, table)
    o_ctx = pl.pallas_call(
        functools.partial(_na_kernel, local=False),
        grid=(nh,),
        in_specs=[pl.BlockSpec((LANES, nc), lambda p: (p, ctx_blk)),
                  pl.BlockSpec((nc, LANES), lambda p: (ctx_blk, p)),
                  pl.BlockSpec((LANES, nc), lambda p: (nh + p, ctx_blk))],
        out_specs=pl.BlockSpec((nc, LANES), lambda p: (0, p)),
        out_shape=jax.ShapeDtypeStruct((nc, width), F32),
        compiler_params=_cparams("parallel"),
        name="na_attn_ctx",
    )(qvt, k, qvt)
    return o_lat, o_ctx


def _rope_tables(n_lat, n_ctx):
    tok = jnp.arange(n_lat)
    row = (tok // GRID_W).astype(F32)
    col = (tok % GRID_W).astype(F32)
    n_freq = HEAD // 4
    inv = ROPE_BASE ** (-jnp.arange(n_freq, dtype=F32) / n_freq)
    ang = jnp.concatenate([row[:, None] * inv, col[:, None] * inv], axis=-1)
    cos, sin = jnp.cos(ang), jnp.sin(ang)
    cos = jnp.concatenate([jnp.tile(cos, (1, 4)), jnp.ones((n_ctx, LANES), F32)], axis=0)
    sin = jnp.concatenate([jnp.concatenate([-sin, sin, -sin, sin], axis=1), jnp.zeros((n_ctx, LANES), F32)], axis=0)
    return cos, sin


def _pick_tm(t, cap):
    return max(m for m in range(8, cap + 1, 8) if t % m == 0)


def kernel(x, c, ctx, c_ctx, ada_w, ada_b, norm_pre, norm_post, ssd_w_in, ssd_conv_w, ssd_conv_b, ssd_dt_bias, ssd_a_log, ssd_d, ssd_norm, ssd_w_out, dif_w_in, dif_lam_q1, dif_lam_k1, dif_lam_q2, dif_lam_k2, dif_subln, dif_w_out, na_w_in, na_rpb, na_w_out):
    n_lat, d = x.shape[1], x.shape[2]
    n_ctx = ctx.shape[1]
    depth = ada_w.shape[0]
    xa = jnp.concatenate([x[0], ctx[0]], axis=0)
    t = xa.shape[0]
    mods = _modulation(c, c_ctx, ada_w, ada_b)
    rope = None
    tm_in = _pick_tm(t, 1280)
    tm_sum = _pick_tm(t, 640)
    tm_lat = _pick_tm(n_lat, 512)
    q_scale = HEAD ** -0.5 * LOG2E
    ia = ib = ic = 0
    for i in range(depth):
        last = i == depth - 1
        out_rows = n_lat if last else t
        shift, scale, gate = (mods[i, 0:2, k * d:(k + 1) * d] for k in range(3))
        nw = norm_pre[i][None, :]
        pw = norm_post[i][None, :]
        inproj = functools.partial(_inproj, xa, nw, scale, shift, n_lat=n_lat, tm=tm_in)
        kind = i % 3
        if kind == 0:
            w_in = ssd_w_in[ia]
            inner = ssd_w_out.shape[1]
            gn2 = w_in.shape[1] - 2 * inner - 2 * SSD_HEADS
            u = inproj(w_in[:, :2 * inner + gn2].astype(MM), out_dtype=F32, tn=512, name="ssd_inproj")
            w_dt = jnp.zeros((d, 2 * LANES), F32)
            w_dt = w_dt.at[:, :SSD_HEADS].set(w_in[:, 2 * inner + gn2:2 * inner + gn2 + SSD_HEADS])
            w_dt = w_dt.at[:, LANES:LANES + SSD_HEADS].set(w_in[:, 2 * inner + gn2 + SSD_HEADS:])
            dtp = inproj(w_dt.astype(MM), out_dtype=F32, tn=2 * LANES, name="ssd_inproj_dt")
            cw, cb = ssd_conv_w[ia], ssd_conv_b[ia][None, :]
            xs = _conv_silu(u, inner, inner, cw[:, :inner], cb[:, :inner], n_lat=n_lat, tm=n_ctx, tc=1024,
                            out_dtype=F32, name="ssd_conv_x")
            bc = _conv_silu(u, 2 * inner, gn2, cw[:, inner:], cb[:, inner:], n_lat=n_lat, tm=n_ctx, tc=1024,
                            out_dtype=MM, name="ssd_conv_bc")
            y = _ssd_scan(xs, bc, dtp, ssd_dt_bias[ia], ssd_a_log[ia], ssd_d[ia], n_lat=n_lat)
            tm = tm_lat if last else tm_sum
            y_ops = [(y, pl.BlockSpec((None, tm, inner), lambda r, dd=dd: (dd, r, 0))) for dd in range(2)]
            z_op = (u, pl.BlockSpec((tm, inner), lambda r: (r, 0)))
            xa = _outproj(y_ops, "sum", z_op, ssd_norm[ia][None, :], ssd_w_out[ia].astype(MM), xa, pw, gate,
                          n_lat=n_lat, tm=tm, out_rows=out_rows, name="ssd_outproj")
            ia += 1
        else:
            if kind == 1:
                w_in, w_out, width = dif_w_in[ib], dif_w_out[ib], dif_w_out.shape[1]
                if rope is None:
                    rope = _rope_tables(n_lat, n_ctx)
                rope_k, rope_q, n_rope = rope, tuple(tab.T for tab in rope), width // 256
            else:
                w_in, w_out, width = na_w_in[ic], na_w_out[ic], d
                rope_k = rope_q = None
                n_rope = 0
            k = inproj(w_in[:, width:2 * width].astype(MM), out_dtype=MM, tn=256, n_rope=n_rope, rope=rope_k,
                       name="attn_inproj_k")
            w_qv = jnp.concatenate([w_in[:, :width], w_in[:, 2 * width:3 * width]], axis=1).T.astype(MM)
            qvt = inproj(w_qv, out_dtype=MM, tn=256, n_q=width // 256, n_rope=n_rope, q_scale=q_scale,
                         rope=rope_q, transposed=True, name="attn_inproj_qvt")
            g = inproj(w_in[:, 3 * width:].astype(MM), out_dtype=F32, tn=512, name="attn_inproj_g")
            if kind == 1:
                lam_init = 0.8 - 0.6 * math.exp(-0.3 * i)
                lam4 = jnp.stack([dif_lam_q1[ib], dif_lam_k1[ib], dif_lam_q2[ib], dif_lam_k2[ib]])
                o_lat, o_ctx = _diff_attention(k, qvt, lam4, dif_subln[ib][None, :], n_lat=n_lat,
                                               lam_init=lam_init, tq=min(1024, n_lat), tk=min(1024, n_lat))
                ib += 1
            else:
                o_lat, o_ctx = _na_attention(k, qvt, na_rpb[ic], n_lat=n_lat)
                ic += 1
            g_op = (g, pl.BlockSpec((n_ctx, width), lambda r: (r, 0)))
            xa = _outproj(_attn_out_ops(o_lat, o_ctx, n_ctx), "select", g_op, None, w_out.astype(MM), xa, pw,
                          gate, n_lat=n_lat, tm=n_ctx, out_rows=out_rows, name="attn_outproj")
    return xa[:n_lat][None]
```

```python
import functools
import math

import numpy as np
import jax
import jax.numpy as jnp
from jax import lax
from jax.experimental import pallas as pl
from jax.experimental.pallas import tpu as pltpu

F32 = jnp.float32
MM = jnp.bfloat16

EPS = 1e-6
GRID_W = 64
LANES = 128
HEAD = 64
NEG = -1e30
LOG2E = math.log2(math.e)

SSD_HEADS = 32
SSD_STATE = 128
SSD_CONV = 5
SSD_CHUNK = 128

ONES_ROWS = 16

NA_KR = 8
NA_KC = 16
NA_RB = 4
NA_WIN = NA_RB + NA_KR
ROPE_BASE = 10000.0

VMEM_LIMIT = 52 * 1024 * 1024


def _cparams(*sem):
    return pltpu.CompilerParams(dimension_semantics=sem, vmem_limit_bytes=VMEM_LIMIT)


def _silu(v):
    return v * (1.0 / (1.0 + jnp.exp(-v)))


def _softplus(v):
    return jnp.maximum(v, 0.0) + jnp.log(1.0 + jnp.exp(-jnp.abs(v)))


def _rms(v, w):
    return v * lax.rsqrt(jnp.mean(v * v, axis=-1, keepdims=True) + EPS) * w


def _dot(a, b):
    return jnp.dot(a, b, preferred_element_type=F32)


def _dot_nt(a, b):
    return lax.dot_general(a, b, (((1,), (1,)), ((), ())), preferred_element_type=F32)


def _lane_lo(shape):
    return lax.broadcasted_iota(jnp.int32, shape, len(shape) - 1) % LANES < HEAD


def _split_heads(qt):
    top = lax.broadcasted_iota(jnp.int32, qt.shape, 0) < HEAD
    zero = jnp.zeros_like(qt)
    return jnp.where(top, qt, zero), jnp.where(top, zero, qt)


def _mod_kernel(cc_ref, w_ref, b_ref, o_ref):
    s = _silu(cc_ref[...])
    o_ref[0] = _dot(s.astype(MM), w_ref[0].astype(MM)) + b_ref[0]


def _modulation(c, c_ctx, ada_w, ada_b):
    depth, d, d3 = ada_w.shape
    cc = jnp.zeros((8, d), F32).at[0].set(c[0]).at[1].set(c_ctx)
    out = pl.pallas_call(
        _mod_kernel,
        grid=(depth, d3 // d),
        in_specs=[pl.BlockSpec((8, d), lambda i, j: (0, 0)),
                  pl.BlockSpec((1, d, d), lambda i, j: (i, 0, j)),
                  pl.BlockSpec((1, 1, d), lambda i, j: (i, 0, j))],
        out_specs=pl.BlockSpec((1, 8, d), lambda i, j: (i, 0, j)),
        out_shape=jax.ShapeDtypeStruct((depth, 8, d3), F32),
        compiler_params=_cparams("parallel", "parallel"),
        name="adaln_mod",
    )(cc, ada_w, ada_b.reshape(depth, 1, d3))
    return out


def _row_select(i, tm, n_lat, v2):
    rows = i * tm + lax.broadcasted_iota(jnp.int32, (tm, 1), 0)
    return jnp.where(rows < n_lat, v2[0:1, :], v2[1:2, :])


def _inproj_kernel(*refs, tm, n_lat, n_q, n_rope, q_scale, transposed):
    if n_rope:
        x_ref, nw_ref, sc_ref, sh_ref, w_ref, cos_ref, sin_ref, o_ref, h_ref = refs
    else:
        x_ref, nw_ref, sc_ref, sh_ref, w_ref, o_ref, h_ref = refs
    i, j = pl.program_id(0), pl.program_id(1)
    fa = 0 if transposed else 1

    @pl.when(j == 0)
    def _():
        h = _rms(x_ref[...], nw_ref[...])
        h = h * (1.0 + _row_select(i, tm, n_lat, sc_ref[...])) + _row_select(i, tm, n_lat, sh_ref[...])
        h_ref[...] = h.astype(MM)

    acc = _dot_nt(w_ref[...], h_ref[...]) if transposed else _dot(h_ref[...], w_ref[...])
    if n_q:
        acc = acc * jnp.where(j < n_q, q_scale, 1.0)
    if n_rope:
        tn = acc.shape[fa]

        @pl.when(j < n_rope)
        def _():
            reps = tn // LANES
            cos = jnp.concatenate([cos_ref[...]] * reps, axis=fa)
            sin = jnp.concatenate([sin_ref[...]] * reps, axis=fa)
            feat = lax.broadcasted_iota(jnp.int32, acc.shape, fa)
            swapped = jnp.where(feat % HEAD < HEAD // 2,
                                pltpu.roll(acc, tn - HEAD // 2, fa), pltpu.roll(acc, HEAD // 2, fa))
            o_ref[...] = (acc * cos + swapped * sin).astype(o_ref.dtype)

        @pl.when(j >= n_rope)
        def _():
            o_ref[...] = acc.astype(o_ref.dtype)
    else:
        o_ref[...] = acc.astype(o_ref.dtype)


def _inproj(xa, nw, sc, sh, w, *, n_lat, out_dtype, tm, tn, n_q=0, n_rope=0, q_scale=1.0, rope=None,
            transposed=False, name="inproj"):
    t, d = xa.shape
    f = w.shape[0] if transposed else w.shape[1]
    assert t % tm == 0 and f % tn == 0
    in_specs = [pl.BlockSpec((tm, d), lambda i, j: (i, 0)),
                pl.BlockSpec((1, d), lambda i, j: (0, 0)),
                pl.BlockSpec((2, d), lambda i, j: (0, 0)),
                pl.BlockSpec((2, d), lambda i, j: (0, 0)),
                pl.BlockSpec((tn, d), lambda i, j: (j, 0)) if transposed else pl.BlockSpec((d, tn), lambda i, j: (0, j))]
    args = [xa, nw, sc, sh, w]
    if n_rope:
        tab = pl.BlockSpec((LANES, tm), lambda i, j: (0, i)) if transposed else pl.BlockSpec((tm, LANES), lambda i, j: (i, 0))
        in_specs += [tab, tab]
        args += list(rope)
    if transposed:
        out_spec, out_shape = pl.BlockSpec((tn, tm), lambda i, j: (j, i)), (f, t)
    else:
        out_spec, out_shape = pl.BlockSpec((tm, tn), lambda i, j: (i, j)), (t, f)
    return pl.pallas_call(
        functools.partial(_inproj_kernel, tm=tm, n_lat=n_lat, n_q=n_q, n_rope=n_rope, q_scale=q_scale,
                          transposed=transposed),
        grid=(t // tm, f // tn),
        in_specs=in_specs,
        out_specs=out_spec,
        out_shape=jax.ShapeDtypeStruct(out_shape, out_dtype),
        scratch_shapes=[pltpu.VMEM((tm, d), MM)],
        compiler_params=_cparams("parallel", "arbitrary"),
        name=name,
    )(*args)


def _outproj_kernel(*refs, tm, n_lat, n_a, combine, use_norm):
    a_refs = refs[:n_a]
    g_ref = refs[n_a]
    rest = refs[n_a + 1:]
    if use_norm:
        nm_ref, rest = rest[0], rest[1:]
    w_ref, x_ref, pw_ref, gate_ref, o_ref = rest
    i = pl.program_id(0)
    if combine == "sum":
        a = a_refs[0][...]
        for r in a_refs[1:]:
            a = a + r[...]
    else:
        a = jnp.where(i * tm < n_lat, a_refs[0][...], a_refs[1][...])
    a = a * _silu(g_ref[...])
    if use_norm:
        a = _rms(a, nm_ref[...])
    o = _dot(a.astype(MM), w_ref[...])
    o = _rms(o, pw_ref[...])
    o_ref[...] = x_ref[...] + _row_select(i, tm, n_lat, gate_ref[...]) * o


def _outproj(a_ops, combine, g_op, norm_w, w, xa, pw, gate, *, n_lat, tm, out_rows, name="outproj"):
    t, d = xa.shape
    f = w.shape[0]
    assert out_rows % tm == 0 and (n_lat % tm == 0 or combine == "sum")
    tok = pl.BlockSpec((tm, d), lambda i: (i, 0))
    full = lambda r, c: pl.BlockSpec((r, c), lambda i: (0, 0))
    in_specs = [spec for _, spec in a_ops] + [g_op[1]]
    args = [arr for arr, _ in a_ops] + [g_op[0]]
    if norm_w is not None:
        in_specs.append(full(1, f))
        args.append(norm_w)
    in_specs += [full(f, d), tok, full(1, d), full(2, d)]
    args += [w, xa, pw, gate]
    return pl.pallas_call(
        functools.partial(_outproj_kernel, tm=tm, n_lat=n_lat, n_a=len(a_ops), combine=combine,
                          use_norm=norm_w is not None),
        grid=(out_rows // tm,),
        in_specs=in_specs,
        out_specs=tok,
        out_shape=jax.ShapeDtypeStruct((out_rows, d), F32),
        compiler_params=_cparams("parallel"),
        name=name,
    )(*args)


def _attn_out_ops(o_lat, o_ctx, tm):
    nl = o_lat.shape[0] // tm
    f = o_lat.shape[1]
    return [(o_lat, pl.BlockSpec((tm, f), lambda i: (jnp.minimum(i, nl - 1), 0))),
            (o_ctx, pl.BlockSpec((tm, f), lambda i: (jnp.maximum(i - nl, 0), 0)))]


def _conv_kernel(prev_ref, cur_ref, next_ref, w_ref, b_ref, o_ref, *, tm, first_blocks, last_blocks):
    i = pl.program_id(0)
    is_first = functools.reduce(jnp.logical_or, [i == b for b in first_blocks])
    is_last = functools.reduce(jnp.logical_or, [i == b for b in last_blocks])
    u = jnp.concatenate([jnp.where(is_first, 0.0, prev_ref[...]), cur_ref[...],
                         jnp.where(is_last, 0.0, next_ref[...])], axis=0)
    rows = u.shape[0]
    tap = [w_ref[k:k + 1, :] * u for k in range(SSD_CONV)]
    later = pltpu.roll(tap[3] + pltpu.roll(tap[4], rows - 1, 0), rows - 1, 0)
    earlier = pltpu.roll(tap[1] + pltpu.roll(tap[0], 1, 0), 1, 0)
    acc = (tap[2] + later + earlier)[8:8 + tm] + b_ref[...]
    o_ref[...] = _silu(acc).astype(o_ref.dtype)


def _conv_silu(u, col0, width, conv_w, conv_b, *, n_lat, tm, tc, out_dtype, name):
    assert SSD_CONV == 5
    t = u.shape[0]
    assert t % tm == 0 and n_lat % tm == 0 and width % tc == 0 and col0 % tc == 0 and tm % 8 == 0
    cb0 = col0 // tc
    r8 = tm // 8
    nb8 = t // 8
    nl, nt = n_lat // tm, t // tm
    return pl.pallas_call(
        functools.partial(_conv_kernel, tm=tm, first_blocks=(0, nl), last_blocks=(nl - 1, nt - 1)),
        grid=(nt, width // tc),
        in_specs=[pl.BlockSpec((8, tc), lambda i, j: (jnp.maximum(i * r8 - 1, 0), cb0 + j)),
                  pl.BlockSpec((tm, tc), lambda i, j: (i, cb0 + j)),
                  pl.BlockSpec((8, tc), lambda i, j: (jnp.minimum((i + 1) * r8, nb8 - 1), cb0 + j)),
                  pl.BlockSpec((SSD_CONV, tc), lambda i, j: (0, j)),
                  pl.BlockSpec((1, tc), lambda i, j: (0, j))],
        out_specs=pl.BlockSpec((tm, tc), lambda i, j: (i, j)),
        out_shape=jax.ShapeDtypeStruct((t, width), out_dtype),
        compiler_params=_cparams("parallel", "parallel"),
        name=name,
    )(u, u, u, conv_w, conv_b)


def _ssd_kernel(x_ref, b_ref, c_ref, dt_ref, bias_ref, alog_ref, dskip_ref, y_ref, h_ref):
    d, s = pl.program_id(0), pl.program_id(1)
    q = SSD_CHUNK

    @pl.when(s == 0)
    def _():
        h_ref[...] = jnp.zeros(h_ref.shape, F32)

    dt = _softplus(dt_ref[...] + bias_ref[...])
    a = dt * (-jnp.exp(alog_ref[...]))
    ii = lax.broadcasted_iota(jnp.int32, (q, q), 0)
    jj = lax.broadcasted_iota(jnp.int32, (q, q), 1)
    keep = jnp.where(d == 0, jj - ii, ii - jj) <= 0
    cs = jnp.dot(keep.astype(F32), a, preferred_element_type=F32, precision=lax.Precision.HIGHEST)
    cs_t = cs.T
    dt_t = dt.T
    tot = jnp.sum(a, axis=0, keepdims=True)
    tot_t = jnp.sum(a.T, axis=1, keepdims=True)
    w_t = jnp.exp(tot_t - cs_t) * dt_t
    dec = jnp.exp(tot)
    lane_lo = _lane_lo((q, LANES))
    skip_on = jnp.where(d == 0, 1.0, 0.0)

    n_groups = b_ref.shape[1] // SSD_STATE
    hpg = SSD_HEADS // n_groups
    for g in range(n_groups):
        bg = b_ref[:, g * SSD_STATE:(g + 1) * SSD_STATE]
        cg = c_ref[:, g * SSD_STATE:(g + 1) * SSD_STATE]
        cb = _dot_nt(cg, bg)
        bg_t = bg.astype(F32).T
        cg32 = cg.astype(F32)
        for pr in range(hpg // 2):
            p = g * (hpg // 2) + pr
            x32 = x_ref[:, p * LANES:(p + 1) * LANES]
            xb = x32.astype(MM)
            hp = h_ref[p]
            lhs_y, lhs_h = [], []
            for half in range(2):
                hd = 2 * p + half
                col = jnp.broadcast_to(cs[:, hd:hd + 1], (q, q))
                row = jnp.broadcast_to(cs_t[hd:hd + 1, :], (q, q))
                lmat = jnp.exp(jnp.where(keep, col - row, NEG))
                m = cb * lmat * dt_t[hd:hd + 1, :]
                ce = cg32 * jnp.exp(col)
                lhs_y.append(jnp.concatenate([m.astype(MM), ce.astype(MM)], axis=1))
                lhs_h.append((bg_t * w_t[hd:hd + 1, :]).astype(MM))
            rhs = jnp.concatenate([xb, hp.astype(MM)], axis=0)
            yf = _dot(jnp.concatenate(lhs_y, axis=0), rhs)
            hf = _dot(jnp.concatenate(lhs_h, axis=0), xb)
            y = jnp.where(lane_lo, yf[:q], yf[q:])
            y_ref[0, :, p * LANES:(p + 1) * LANES] = y + skip_on * dskip_ref[:, p * LANES:(p + 1) * LANES] * x32
            hn = jnp.where(lane_lo, hf[:SSD_STATE], hf[SSD_STATE:])
            hd0 = 2 * p
            dec_p = jnp.where(lane_lo[0:1, :], dec[:, hd0:hd0 + 1], dec[:, hd0 + 1:hd0 + 2])
            h_ref[p] = hp * dec_p + hn


def _ssd_scan(xs, bc, dtp, dt_bias, a_log, d_skip, *, n_lat):
    t, inner = xs.shape
    gn = bc.shape[1] // 2
    q = SSD_CHUNK
    nlc, ntc = n_lat // q, t // q

    def chunk(d, s):
        fwd = jnp.where(s < ntc - nlc, nlc + s, s - (ntc - nlc))
        return jnp.where(d == 0, fwd, ntc - 1 - s)

    pad = lambda v: jnp.zeros((2, 1, LANES), F32).at[:, 0, :SSD_HEADS].set(v)
    dsk = jnp.repeat(d_skip, inner // SSD_HEADS)[None, :]
    return pl.pallas_call(
        _ssd_kernel,
        grid=(2, ntc),
        in_specs=[pl.BlockSpec((q, inner), lambda d, s: (chunk(d, s), 0)),
                  pl.BlockSpec((q, gn), lambda d, s: (chunk(d, s), 0)),
                  pl.BlockSpec((q, gn), lambda d, s: (chunk(d, s), 1)),
                  pl.BlockSpec((q, LANES), lambda d, s: (chunk(d, s), d)),
                  pl.BlockSpec((None, 1, LANES), lambda d, s: (d, 0, 0)),
                  pl.BlockSpec((None, 1, LANES), lambda d, s: (d, 0, 0)),
                  pl.BlockSpec((1, inner), lambda d, s: (0, 0))],
        out_specs=pl.BlockSpec((1, q, inner), lambda d, s: (d, chunk(d, s), 0)),
        out_shape=jax.ShapeDtypeStruct((2, t, inner), F32),
        scratch_shapes=[pltpu.VMEM((SSD_HEADS // 2, SSD_STATE, LANES), F32)],
        compiler_params=_cparams("parallel", "arbitrary"),
        name="ssd_scan",
    )(xs, bc, bc, dtp, pad(dt_bias), pad(a_log), dsk)


def _diff_kernel(*refs, stream, nk, lam_init):
    if stream:
        (qt_ref, kc_ref, vct_ref, k_ref, vt_ref, lam_ref, sub_ref, o_ref,
         qs_ref, m_ref, l_ref, acc_ref, s0_ref, s1_ref, mt0_ref, mt1_ref) = refs
    else:
        qt_ref, kc_ref, vct_ref, lam_ref, sub_ref, o_ref, qs_ref, m_ref, l_ref, acc_ref = refs
    j = pl.program_id(2)
    nj = pl.num_programs(2)

    tq = qt_ref.shape[1]
    qc = min(tq, 512)
    chunks = [(i, slice(c, c + qc)) for i in range(2) for c in range(0, tq, qc)]

    def scores(k, s_ref, mt_ref):
        for i, cs in chunks:
            s = _dot(k, qs_ref[i, :, cs])
            s_ref[i, :, cs] = s
            mt_ref[i, :, cs] = jnp.max(s, axis=0, keepdims=True)

    def consume(vt, s_ref, mt_ref):
        vt1 = jnp.concatenate([vt, jnp.ones((ONES_ROWS, vt.shape[1]), MM)], axis=0)
        for i, cs in chunks:
            m_old = m_ref[i, :, cs]
            m_new = jnp.maximum(m_old, mt_ref[i, :, cs])
            alpha = jnp.exp2(m_old - m_new)
            p = jnp.exp2(s_ref[i, :, cs] - m_new)
            r = _dot(vt1, p.astype(MM))
            l_ref[i, :, cs] = alpha * l_ref[i, :, cs] + r[LANES:LANES + 1]
            acc_ref[i, :, cs] = alpha * acc_ref[i, :, cs] + r[:LANES]
            m_ref[i, :, cs] = m_new

    @pl.when(j == 0)
    def _():
        qs_ref[0], qs_ref[1] = _split_heads(qt_ref[...])
        kc = kc_ref[...]
        for i in range(2):
            s = _dot(kc, qs_ref[i])
            m_new = jnp.max(s, axis=0, keepdims=True)
            p = jnp.exp2(s - m_new)
            l_ref[i] = jnp.sum(p, axis=0, keepdims=True)
            acc_ref[i] = _dot(vct_ref[...], p.astype(MM))
            m_ref[i] = m_new
        if stream:
            scores(k_ref[...], s0_ref, mt0_ref)

    if stream:
        bufs = ((s0_ref, mt0_ref), (s1_ref, mt1_ref))
        for par in range(2):
            @pl.when((j > 0) & (j < nk) & (j % 2 == par))
            def _(par=par):
                scores(k_ref[...], *bufs[par])
                consume(vt_ref[...], *bufs[1 - par])

        @pl.when(j == nk)
        def _():
            consume(vt_ref[...], *bufs[(nk - 1) % 2])

    @pl.when(j == nj - 1)
    def _():
        lam4 = lam_ref[...]
        lam = (jnp.exp(jnp.sum(lam4[0:1] * lam4[1:2], axis=1, keepdims=True))
               - jnp.exp(jnp.sum(lam4[2:3] * lam4[3:4], axis=1, keepdims=True)) + lam_init)
        o_t = acc_ref[0] * (1.0 / l_ref[0]) - lam * (acc_ref[1] * (1.0 / l_ref[1]))
        o_ref[...] = _rms(o_t.T, sub_ref[...]) * (1.0 - lam_init)


def _diff_attention(k, qvt, lam4, subln, *, n_lat, lam_init, tq, tk):
    t, width = k.shape
    nc = t - n_lat
    nh = width // LANES
    ctx_blk = n_lat // nc

    def call(stream, q_rows, q_blk0, tq_, nk):
        in_specs = [pl.BlockSpec((LANES, tq_), lambda h, i, j: (h, q_blk0 + i)),
                    pl.BlockSpec((nc, LANES), lambda h, i, j: (ctx_blk, h)),
                    pl.BlockSpec((LANES, nc), lambda h, i, j: (nh + h, ctx_blk))]
        args = [qvt, k, qvt]
        scratch = [pltpu.VMEM((2, LANES, tq_), MM), pltpu.VMEM((2, 1, tq_), F32),
                   pltpu.VMEM((2, 1, tq_), F32), pltpu.VMEM((2, LANES, tq_), F32)]
        if stream:
            in_specs += [pl.BlockSpec((tk, LANES), lambda h, i, j: (jnp.minimum(j, nk - 1), h)),
                         pl.BlockSpec((LANES, tk), lambda h, i, j: (nh + h, jnp.maximum(j - 1, 0)))]
            args += [k, qvt]
            scratch += [pltpu.VMEM((2, tk, tq_), F32)] * 2 + [pltpu.VMEM((2, 1, tq_), F32)] * 2
        in_specs += [pl.BlockSpec((4, HEAD), lambda h, i, j: (0, 0)),
                     pl.BlockSpec((1, LANES), lambda h, i, j: (0, 0))]
        args += [lam4, subln]
        return pl.pallas_call(
            functools.partial(_diff_kernel, stream=stream, nk=nk, lam_init=lam_init),
            grid=(nh, q_rows // tq_, nk + 1 if stream else 1),
            in_specs=in_specs,
            out_specs=pl.BlockSpec((tq_, LANES), lambda h, i, j: (i, h)),
            out_shape=jax.ShapeDtypeStruct((q_rows, width), F32),
            scratch_shapes=scratch,
            compiler_params=_cparams("parallel", "parallel", "arbitrary"),
            name="diff_attn" if stream else "diff_attn_ctx",
        )(*args)

    assert n_lat % tq == 0 and n_lat % tk == 0 and n_lat % nc == 0
    return call(True, n_lat, 0, tq, n_lat // tk), call(False, nc, ctx_blk, nc, 0)


def _na_kernel(*refs, local):
    if local:
        qt_ref, kw_ref, vwt_ref, kc_ref, vct_ref, bias_ref, o_ref = refs
    else:
        qt_ref, kc_ref, vct_ref, o_ref = refs
    outs = []
    for half, qh in enumerate(_split_heads(qt_ref[...])):
        s_c = _dot(kc_ref[...], qh)
        m = jnp.max(s_c, axis=0, keepdims=True)
        if local:
            s_l = _dot(kw_ref[...], qh) + bias_ref[half]
            m = jnp.maximum(m, jnp.max(s_l, axis=0, keepdims=True))
        p_c = jnp.exp2(s_c - m)
        l = jnp.sum(p_c, axis=0, keepdims=True)
        o_t = _dot(vct_ref[...], p_c.astype(MM))
        if local:
            p_l = jnp.exp2(s_l - m)
            l = l + jnp.sum(p_l, axis=0, keepdims=True)
            o_t = o_t + _dot(vwt_ref[...], p_l.astype(MM))
        outs.append(o_t * (1.0 / l))
    top = lax.broadcasted_iota(jnp.int32, outs[0].shape, 0) < HEAD
    o_ref[...] = jnp.where(top, outs[0], outs[1]).T


def _na_geometry(rows):
    kr = min(NA_KR, rows)
    assert kr == NA_KR and rows % NA_RB == 0 and rows >= NA_WIN and (rows - NA_WIN) % 2 == 0
    nb = rows // NA_RB
    ws = np.clip(NA_RB * np.arange(nb) - kr // 2, 0, rows - NA_WIN)

    def rowpart(b):
        r = NA_RB * b + np.arange(NA_RB)
        r0 = np.clip(r - kr // 2, 0, rows - kr)
        key = ws[b] + np.arange(NA_WIN)
        valid = (key[:, None] >= r0[None, :]) & (key[:, None] < r0[None, :] + kr)
        brow = np.clip(key[:, None] - r[None, :] + (NA_KR - 1), 0, 2 * NA_KR - 2)
        return valid, brow

    variant_blocks = [0, min(1, nb - 1), nb - 1]
    variant_of = np.where(np.arange(nb) == 0, 0, np.where(np.arange(nb) == nb - 1, 2, 1))
    for b in range(nb):
        vb, bb = rowpart(b)
        vv, bv = rowpart(variant_blocks[variant_of[b]])
        assert (vb == vv).all() and (np.where(vb, bb, 0) == np.where(vv, bv, 0)).all()
    return nb, ws, [rowpart(b) for b in variant_blocks]


def _na_bias_table(rpb, rows):
    _, _, parts = _na_geometry(rows)
    kw = min(NA_KC, GRID_W)
    w = np.arange(GRID_W)
    c0 = np.clip(w - kw // 2, 0, GRID_W - kw)
    kc = np.arange(GRID_W)
    cvalid = (kc[:, None] >= c0[None, :]) & (kc[:, None] < c0[None, :] + kw)
    bcol = np.clip(kc[:, None] - w[None, :] + (NA_KC - 1), 0, 2 * NA_KC - 2)
    onehot = jnp.asarray(bcol[:, :, None] == np.arange(2 * NA_KC - 1)[None, None, :], F32)
    tabs = []
    for rvalid, brow in parts:
        rows_sel = rpb[:, brow]
        tab = jnp.einsum('hkrb,cwb->hkcrw', rows_sel, onehot, precision=lax.Precision.HIGHEST)
        valid = rvalid[:, None, :, None] & cvalid[None, :, None, :]
        tab = jnp.where(valid[None], tab * LOG2E, NEG)
        tabs.append(tab.reshape(rpb.shape[0], NA_WIN * GRID_W, NA_RB * GRID_W))
    return jnp.stack(tabs).astype(F32)


def _na_attention(k, qvt, rpb, *, n_lat):
    t, width = k.shape
    nc = t - n_lat
    nh = width // LANES
    rows = n_lat // GRID_W
    ctx_blk = n_lat // nc
    nb, ws_np, _ = _na_geometry(rows)
    table = _na_bias_table(rpb, rows)
    nq, win = NA_RB * GRID_W, NA_WIN * GRID_W
    assert nc % nq == 0

    def ws(b):
        return jnp.clip(NA_RB // 2 * b - NA_KR // 4, 0, (rows - NA_WIN) // 2) * (2 * GRID_W)

    def variant(b):
        return jnp.where(b == 0, 0, jnp.where(b == nb - 1, 2, 1))

    assert (ws_np == np.clip(NA_RB * np.arange(nb) - NA_KR // 2, 0, rows - NA_WIN)).all()
    o_lat = pl.pallas_call(
        functools.partial(_na_kernel, local=True),
        grid=(nh, nb),
        in_specs=[pl.BlockSpec((LANES, nq), lambda p, b: (p, b)),
                  pl.BlockSpec((pl.Element(win), pl.Element(LANES)), lambda p, b: (ws(b), p * LANES)),
                  pl.BlockSpec((pl.Element(LANES), pl.Element(win)), lambda p, b: ((nh + p) * LANES, ws(b))),
                  pl.BlockSpec((nc, LANES), lambda p, b: (ctx_blk, p)),
                  pl.BlockSpec((LANES, nc), lambda p, b: (nh + p, ctx_blk)),
                  pl.BlockSpec((None, 2, win, nq), lambda p, b: (variant(b), p, 0, 0))],
        out_specs=pl.BlockSpec((nq, LANES), lambda p, b: (b, p)),
        out_shape=jax.ShapeDtypeStruct((n_lat, width), F32),
        compiler_params=_cparams("parallel", "parallel"),
        name="na_attn",
    )(qvt, k, qvt, k, qvt, table)
    o_ctx = pl.pallas_call(
        functools.partial(_na_kernel, local=False),
        grid=(nh,),
        in_specs=[pl.BlockSpec((LANES, nc), lambda p: (p, ctx_blk)),
                  pl.BlockSpec((nc, LANES), lambda p: (ctx_blk, p)),
                  pl.BlockSpec((LANES, nc), lambda p: (nh + p, ctx_blk))],
        out_specs=pl.BlockSpec((nc, LANES), lambda p: (0, p)),
        out_shape=jax.ShapeDtypeStruct((nc, width), F32),
        compiler_params=_cparams("parallel"),
        name="na_attn_ctx",
    )(qvt, k, qvt)
    return o_lat, o_ctx


def _rope_tables(n_lat, n_ctx):
    tok = jnp.arange(n_lat)
    row = (tok // GRID_W).astype(F32)
    col = (tok % GRID_W).astype(F32)
    n_freq = HEAD // 4
    inv = ROPE_BASE ** (-jnp.arange(n_freq, dtype=F32) / n_freq)
    ang = jnp.concatenate([row[:, None] * inv, col[:, None] * inv], axis=-1)
    cos, sin = jnp.cos(ang), jnp.sin(ang)
    cos = jnp.concatenate([jnp.tile(cos, (1, 4)), jnp.ones((n_ctx, LANES), F32)], axis=0)
    sin = jnp.concatenate([jnp.concatenate([-sin, sin, -sin, sin], axis=1), jnp.zeros((n_ctx, LANES), F32)], axis=0)
    return cos, sin


def _pick_tm(t, cap):
    return max(m for m in range(8, cap + 1, 8) if t % m == 0)


def kernel(x, c, ctx, c_ctx, ada_w, ada_b, norm_pre, norm_post, ssd_w_in, ssd_conv_w, ssd_conv_b, ssd_dt_bias, ssd_a_log, ssd_d, ssd_norm, ssd_w_out, dif_w_in, dif_lam_q1, dif_lam_k1, dif_lam_q2, dif_lam_k2, dif_subln, dif_w_out, na_w_in, na_rpb, na_w_out):
    n_lat, d = x.shape[1], x.shape[2]
    n_ctx = ctx.shape[1]
    depth = ada_w.shape[0]
    xa = jnp.concatenate([x[0], ctx[0]], axis=0)
    t = xa.shape[0]
    mods = _modulation(c, c_ctx, ada_w, ada_b)
    rope = None
    tm_in = _pick_tm(t, 1280)
    tm_sum = _pick_tm(t, 640)
    tm_lat = _pick_tm(n_lat, 512)
    q_scale = HEAD ** -0.5 * LOG2E
    ia = ib = ic = 0
    for i in range(depth):
        last = i == depth - 1
        out_rows = n_lat if last else t
        shift, scale, gate = (mods[i, 0:2, k * d:(k + 1) * d] for k in range(3))
        nw = norm_pre[i][None, :]
        pw = norm_post[i][None, :]
        inproj = functools.partial(_inproj, xa, nw, scale, shift, n_lat=n_lat, tm=tm_in)
        kind = i % 3
        if kind == 0:
            w_in = ssd_w_in[ia]
            inner = ssd_w_out.shape[1]
            gn2 = w_in.shape[1] - 2 * inner - 2 * SSD_HEADS
            u = inproj(w_in[:, :2 * inner + gn2].astype(MM), out_dtype=F32, tn=512, name="ssd_inproj")
            w_dt = jnp.zeros((d, 2 * LANES), F32)
            w_dt = w_dt.at[:, :SSD_HEADS].set(w_in[:, 2 * inner + gn2:2 * inner + gn2 + SSD_HEADS])
            w_dt = w_dt.at[:, LANES:LANES + SSD_HEADS].set(w_in[:, 2 * inner + gn2 + SSD_HEADS:])
            dtp = inproj(w_dt.astype(MM), out_dtype=F32, tn=2 * LANES, name="ssd_inproj_dt")
            cw, cb = ssd_conv_w[ia], ssd_conv_b[ia][None, :]
            xs = _conv_silu(u, inner, inner, cw[:, :inner], cb[:, :inner], n_lat=n_lat, tm=n_ctx, tc=1024,
                            out_dtype=F32, name="ssd_conv_x")
            bc = _conv_silu(u, 2 * inner, gn2, cw[:, inner:], cb[:, inner:], n_lat=n_lat, tm=n_ctx, tc=1024,
                            out_dtype=MM, name="ssd_conv_bc")
            y = _ssd_scan(xs, bc, dtp, ssd_dt_bias[ia], ssd_a_log[ia], ssd_d[ia], n_lat=n_lat)
            tm = tm_lat if last else tm_sum
            y_ops = [(y, pl.BlockSpec((None, tm, inner), lambda r, dd=dd: (dd, r, 0))) for dd in range(2)]
            z_op = (u, pl.BlockSpec((tm, inner), lambda r: (r, 0)))
            xa = _outproj(y_ops, "sum", z_op, ssd_norm[ia][None, :], ssd_w_out[ia].astype(MM), xa, pw, gate,
                          n_lat=n_lat, tm=tm, out_rows=out_rows, name="ssd_outproj")
            ia += 1
        else:
            if kind == 1:
                w_in, w_out, width = dif_w_in[ib], dif_w_out[ib], dif_w_out.shape[1]
                if rope is None:
                    rope = _rope_tables(n_lat, n_ctx)
                rope_k, rope_q, n_rope = rope, tuple(tab.T for tab in rope), width // 256
            else:
                w_in, w_out, width = na_w_in[ic], na_w_out[ic], d
                rope_k = rope_q = None
                n_rope = 0
            k = inproj(w_in[:, width:2 * width].astype(MM), out_dtype=MM, tn=256, n_rope=n_rope, rope=rope_k,
                       name="attn_inproj_k")
            w_qv = jnp.concatenate([w_in[:, :width], w_in[:, 2 * width:3 * width]], axis=1).T.astype(MM)
            qvt = inproj(w_qv, out_dtype=MM, tn=256, n_q=width // 256, n_rope=n_rope, q_scale=q_scale,
                         rope=rope_q, transposed=True, name="attn_inproj_qvt")
            g = inproj(w_in[:, 3 * width:].astype(MM), out_dtype=F32, tn=512, name="attn_inproj_g")
            if kind == 1:
                lam_init = 0.8 - 0.6 * math.exp(-0.3 * i)
                lam4 = jnp.stack([dif_lam_q1[ib], dif_lam_k1[ib], dif_lam_q2[ib], dif_lam_k2[ib]])
                o_lat, o_ctx = _diff_attention(k, qvt, lam4, dif_subln[ib][None, :], n_lat=n_lat,
                                               lam_init=lam_init, tq=min(1024, n_lat), tk=min(2048, n_lat))
                ib += 1
            else:
                o_lat, o_ctx = _na_attention(k, qvt, na_rpb[ic], n_lat=n_lat)
                ic += 1
            g_op = (g, pl.BlockSpec((n_ctx, width), lambda r: (r, 0)))
            xa = _outproj(_attn_out_ops(o_lat, o_ctx, n_ctx), "select", g_op, None, w_out.astype(MM), xa, pw,
                          gate, n_lat=n_lat, tm=n_ctx, out_rows=out_rows, name="attn_outproj")
    return xa[:n_lat][None]
```

```python
import functools
import math

import numpy as np
import jax
import jax.numpy as jnp
from jax import lax
from jax.experimental import pallas as pl
from jax.experimental.pallas import tpu as pltpu

F32 = jnp.float32
MM = jnp.bfloat16

EPS = 1e-6
GRID_W = 64
LANES = 128
HEAD = 64
NEG = -1e30
LOG2E = math.log2(math.e)

SSD_HEADS = 32
SSD_STATE = 128
SSD_CONV = 5
SSD_CHUNK = 128

ONES_ROWS = 16

NA_KR = 8
NA_KC = 16
NA_RB = 4
NA_WIN = NA_RB + NA_KR
ROPE_BASE = 10000.0

VMEM_LIMIT = 52 * 1024 * 1024
TN_WIDE = 1024
TN_QKV = 512


def _cparams(*sem):
    return pltpu.CompilerParams(dimension_semantics=sem, vmem_limit_bytes=VMEM_LIMIT)


def _silu(v):
    return v * (1.0 / (1.0 + jnp.exp(-v)))


def _softplus(v):
    return jnp.maximum(v, 0.0) + jnp.log(1.0 + jnp.exp(-jnp.abs(v)))


def _rms(v, w):
    return v * lax.rsqrt(jnp.mean(v * v, axis=-1, keepdims=True) + EPS) * w


def _dot(a, b):
    return jnp.dot(a, b, preferred_element_type=F32)


def _dot_nt(a, b):
    return lax.dot_general(a, b, (((1,), (1,)), ((), ())), preferred_element_type=F32)


def _lane_lo(shape):
    return lax.broadcasted_iota(jnp.int32, shape, len(shape) - 1) % LANES < HEAD


def _split_heads(qt):
    top = lax.broadcasted_iota(jnp.int32, qt.shape, 0) < HEAD
    zero = jnp.zeros_like(qt)
    return jnp.where(top, qt, zero), jnp.where(top, zero, qt)


def _mod_kernel(cc_ref, w_ref, b_ref, o_ref):
    s = _silu(cc_ref[...])
    o_ref[0] = _dot(s.astype(MM), w_ref[0].astype(MM)) + b_ref[0]


def _modulation(c, c_ctx, ada_w, ada_b):
    depth, d, d3 = ada_w.shape
    cc = jnp.zeros((8, d), F32).at[0].set(c[0]).at[1].set(c_ctx)
    out = pl.pallas_call(
        _mod_kernel,
        grid=(depth, d3 // d),
        in_specs=[pl.BlockSpec((8, d), lambda i, j: (0, 0)),
                  pl.BlockSpec((1, d, d), lambda i, j: (i, 0, j)),
                  pl.BlockSpec((1, 1, d), lambda i, j: (i, 0, j))],
        out_specs=pl.BlockSpec((1, 8, d), lambda i, j: (i, 0, j)),
        out_shape=jax.ShapeDtypeStruct((depth, 8, d3), F32),
        compiler_params=_cparams("parallel", "parallel"),
        name="adaln_mod",
    )(cc, ada_w, ada_b.reshape(depth, 1, d3))
    return out


def _row_select(i, tm, n_lat, v2):
    rows = i * tm + lax.broadcasted_iota(jnp.int32, (tm, 1), 0)
    return jnp.where(rows < n_lat, v2[0:1, :], v2[1:2, :])


def _inproj_kernel(*refs, tm, n_lat, n_q, n_rope, q_scale, transposed):
    if n_rope:
        x_ref, nw_ref, sc_ref, sh_ref, w_ref, cos_ref, sin_ref, o_ref, h_ref = refs
    else:
        x_ref, nw_ref, sc_ref, sh_ref, w_ref, o_ref, h_ref = refs
    i, j = pl.program_id(0), pl.program_id(1)
    fa = 0 if transposed else 1

    @pl.when(j == 0)
    def _():
        h = _rms(x_ref[...], nw_ref[...])
        h = h * (1.0 + _row_select(i, tm, n_lat, sc_ref[...])) + _row_select(i, tm, n_lat, sh_ref[...])
        h_ref[...] = h.astype(MM)

    acc = _dot_nt(w_ref[...], h_ref[...]) if transposed else _dot(h_ref[...], w_ref[...])
    if n_q:
        acc = acc * jnp.where(j < n_q, q_scale, 1.0)
    if n_rope:
        tn = acc.shape[fa]

        @pl.when(j < n_rope)
        def _():
            reps = tn // LANES
            cos = jnp.concatenate([cos_ref[...]] * reps, axis=fa)
            sin = jnp.concatenate([sin_ref[...]] * reps, axis=fa)
            feat = lax.broadcasted_iota(jnp.int32, acc.shape, fa)
            swapped = jnp.where(feat % HEAD < HEAD // 2,
                                pltpu.roll(acc, tn - HEAD // 2, fa), pltpu.roll(acc, HEAD // 2, fa))
            o_ref[...] = (acc * cos + swapped * sin).astype(o_ref.dtype)

        @pl.when(j >= n_rope)
        def _():
            o_ref[...] = acc.astype(o_ref.dtype)
    else:
        o_ref[...] = acc.astype(o_ref.dtype)


def _inproj(xa, nw, sc, sh, w, *, n_lat, out_dtype, tm, tn, n_q=0, n_rope=0, q_scale=1.0, rope=None,
            transposed=False, name="inproj"):
    t, d = xa.shape
    f = w.shape[0] if transposed else w.shape[1]
    assert t % tm == 0 and f % tn == 0
    in_specs = [pl.BlockSpec((tm, d), lambda i, j: (i, 0)),
                pl.BlockSpec((1, d), lambda i, j: (0, 0)),
                pl.BlockSpec((2, d), lambda i, j: (0, 0)),
                pl.BlockSpec((2, d), lambda i, j: (0, 0)),
                pl.BlockSpec((tn, d), lambda i, j: (j, 0)) if transposed else pl.BlockSpec((d, tn), lambda i, j: (0, j))]
    args = [xa, nw, sc, sh, w]
    if n_rope:
        tab = pl.BlockSpec((LANES, tm), lambda i, j: (0, i)) if transposed else pl.BlockSpec((tm, LANES), lambda i, j: (i, 0))
        in_specs += [tab, tab]
        args += list(rope)
    if transposed:
        out_spec, out_shape = pl.BlockSpec((tn, tm), lambda i, j: (j, i)), (f, t)
    else:
        out_spec, out_shape = pl.BlockSpec((tm, tn), lambda i, j: (i, j)), (t, f)
    return pl.pallas_call(
        functools.partial(_inproj_kernel, tm=tm, n_lat=n_lat, n_q=n_q, n_rope=n_rope, q_scale=q_scale,
                          transposed=transposed),
        grid=(t // tm, f // tn),
        in_specs=in_specs,
        out_specs=out_spec,
        out_shape=jax.ShapeDtypeStruct(out_shape, out_dtype),
        scratch_shapes=[pltpu.VMEM((tm, d), MM)],
        compiler_params=_cparams("parallel", "arbitrary"),
        name=name,
    )(*args)


def _outproj_kernel(*refs, tm, n_lat, n_a, combine, use_norm):
    a_refs = refs[:n_a]
    g_ref = refs[n_a]
    rest = refs[n_a + 1:]
    if use_norm:
        nm_ref, rest = rest[0], rest[1:]
    w_ref, x_ref, pw_ref, gate_ref, o_ref = rest
    i = pl.program_id(0)
    if combine == "sum":
        a = a_refs[0][...]
        for r in a_refs[1:]:
            a = a + r[...]
    else:
        a = jnp.where(i * tm < n_lat, a_refs[0][...], a_refs[1][...])
    a = a * _silu(g_ref[...])
    if use_norm:
        a = _rms(a, nm_ref[...])
    o = _dot(a.astype(MM), w_ref[...])
    o = _rms(o, pw_ref[...])
    o_ref[...] = x_ref[...] + _row_select(i, tm, n_lat, gate_ref[...]) * o


def _outproj(a_ops, combine, g_op, norm_w, w, xa, pw, gate, *, n_lat, tm, out_rows, name="outproj"):
    t, d = xa.shape
    f = w.shape[0]
    assert out_rows % tm == 0 and (n_lat % tm == 0 or combine == "sum")
    tok = pl.BlockSpec((tm, d), lambda i: (i, 0))
    full = lambda r, c: pl.BlockSpec((r, c), lambda i: (0, 0))
    in_specs = [spec for _, spec in a_ops] + [g_op[1]]
    args = [arr for arr, _ in a_ops] + [g_op[0]]
    if norm_w is not None:
        in_specs.append(full(1, f))
        args.append(norm_w)
    in_specs += [full(f, d), tok, full(1, d), full(2, d)]
    args += [w, xa, pw, gate]
    return pl.pallas_call(
        functools.partial(_outproj_kernel, tm=tm, n_lat=n_lat, n_a=len(a_ops), combine=combine,
                          use_norm=norm_w is not None),
        grid=(out_rows // tm,),
        in_specs=in_specs,
        out_specs=tok,
        out_shape=jax.ShapeDtypeStruct((out_rows, d), F32),
        compiler_params=_cparams("parallel"),
        name=name,
    )(*args)


def _attn_out_ops(o_lat, o_ctx, tm):
    nl = o_lat.shape[0] // tm
    f = o_lat.shape[1]
    return [(o_lat, pl.BlockSpec((tm, f), lambda i: (jnp.minimum(i, nl - 1), 0))),
            (o_ctx, pl.BlockSpec((tm, f), lambda i: (jnp.maximum(i - nl, 0), 0)))]


def _conv_kernel(prev_ref, cur_ref, next_ref, w_ref, b_ref, o_ref, *, tm, first_blocks, last_blocks):
    i = pl.program_id(0)
    is_first = functools.reduce(jnp.logical_or, [i == b for b in first_blocks])
    is_last = functools.reduce(jnp.logical_or, [i == b for b in last_blocks])
    u = jnp.concatenate([jnp.where(is_first, 0.0, prev_ref[...]), cur_ref[...],
                         jnp.where(is_last, 0.0, next_ref[...])], axis=0)
    rows = u.shape[0]
    tap = [w_ref[k:k + 1, :] * u for k in range(SSD_CONV)]
    later = pltpu.roll(tap[3] + pltpu.roll(tap[4], rows - 1, 0), rows - 1, 0)
    earlier = pltpu.roll(tap[1] + pltpu.roll(tap[0], 1, 0), 1, 0)
    acc = (tap[2] + later + earlier)[8:8 + tm] + b_ref[...]
    o_ref[...] = _silu(acc).astype(o_ref.dtype)


def _conv_silu(u, col0, width, conv_w, conv_b, *, n_lat, tm, tc, out_dtype, name):
    assert SSD_CONV == 5
    t = u.shape[0]
    assert t % tm == 0 and n_lat % tm == 0 and width % tc == 0 and col0 % tc == 0 and tm % 8 == 0
    cb0 = col0 // tc
    r8 = tm // 8
    nb8 = t // 8
    nl, nt = n_lat // tm, t // tm
    return pl.pallas_call(
        functools.partial(_conv_kernel, tm=tm, first_blocks=(0, nl), last_blocks=(nl - 1, nt - 1)),
        grid=(nt, width // tc),
        in_specs=[pl.BlockSpec((8, tc), lambda i, j: (jnp.maximum(i * r8 - 1, 0), cb0 + j)),
                  pl.BlockSpec((tm, tc), lambda i, j: (i, cb0 + j)),
                  pl.BlockSpec((8, tc), lambda i, j: (jnp.minimum((i + 1) * r8, nb8 - 1), cb0 + j)),
                  pl.BlockSpec((SSD_CONV, tc), lambda i, j: (0, j)),
                  pl.BlockSpec((1, tc), lambda i, j: (0, j))],
        out_specs=pl.BlockSpec((tm, tc), lambda i, j: (i, j)),
        out_shape=jax.ShapeDtypeStruct((t, width), out_dtype),
        compiler_params=_cparams("parallel", "parallel"),
        name=name,
    )(u, u, u, conv_w, conv_b)


def _ssd_kernel(x_ref, b_ref, c_ref, dt_ref, bias_ref, alog_ref, dskip_ref, y_ref, h_ref):
    d, s = pl.program_id(0), pl.program_id(1)
    q = SSD_CHUNK

    @pl.when(s == 0)
    def _():
        h_ref[...] = jnp.zeros(h_ref.shape, F32)

    dt = _softplus(dt_ref[...] + bias_ref[...])
    a = dt * (-jnp.exp(alog_ref[...]))
    ii = lax.broadcasted_iota(jnp.int32, (q, q), 0)
    jj = lax.broadcasted_iota(jnp.int32, (q, q), 1)
    keep = jnp.where(d == 0, jj - ii, ii - jj) <= 0
    cs = jnp.dot(keep.astype(F32), a, preferred_element_type=F32, precision=lax.Precision.HIGHEST)
    cs_t = cs.T
    dt_t = dt.T
    tot = jnp.sum(a, axis=0, keepdims=True)
    tot_t = jnp.sum(a.T, axis=1, keepdims=True)
    w_t = jnp.exp(tot_t - cs_t) * dt_t
    dec = jnp.exp(tot)
    lane_lo = _lane_lo((q, LANES))
    skip_on = jnp.where(d == 0, 1.0, 0.0)

    n_groups = b_ref.shape[1] // SSD_STATE
    hpg = SSD_HEADS // n_groups
    for g in range(n_groups):
        bg = b_ref[:, g * SSD_STATE:(g + 1) * SSD_STATE]
        cg = c_ref[:, g * SSD_STATE:(g + 1) * SSD_STATE]
        cb = _dot_nt(cg, bg)
        bg_t = bg.astype(F32).T
        cg32 = cg.astype(F32)
        for pr in range(hpg // 2):
            p = g * (hpg // 2) + pr
            x32 = x_ref[:, p * LANES:(p + 1) * LANES]
            xb = x32.astype(MM)
            hp = h_ref[p]
            lhs_y, lhs_h = [], []
            for half in range(2):
                hd = 2 * p + half
                col = jnp.broadcast_to(cs[:, hd:hd + 1], (q, q))
                row = jnp.broadcast_to(cs_t[hd:hd + 1, :], (q, q))
                lmat = jnp.exp(jnp.where(keep, col - row, NEG))
                m = cb * lmat * dt_t[hd:hd + 1, :]
                ce = cg32 * jnp.exp(col)
                lhs_y.append(jnp.concatenate([m.astype(MM), ce.astype(MM)], axis=1))
                lhs_h.append((bg_t * w_t[hd:hd + 1, :]).astype(MM))
            rhs = jnp.concatenate([xb, hp.astype(MM)], axis=0)
            yf = _dot(jnp.concatenate(lhs_y, axis=0), rhs)
            hf = _dot(jnp.concatenate(lhs_h, axis=0), xb)
            y = jnp.where(lane_lo, yf[:q], yf[q:])
            y_ref[0, :, p * LANES:(p + 1) * LANES] = y + skip_on * dskip_ref[:, p * LANES:(p + 1) * LANES] * x32
            hn = jnp.where(lane_lo, hf[:SSD_STATE], hf[SSD_STATE:])
            hd0 = 2 * p
            dec_p = jnp.where(lane_lo[0:1, :], dec[:, hd0:hd0 + 1], dec[:, hd0 + 1:hd0 + 2])
            h_ref[p] = hp * dec_p + hn


def _ssd_scan(xs, bc, dtp, dt_bias, a_log, d_skip, *, n_lat):
    t, inner = xs.shape
    gn = bc.shape[1] // 2
    q = SSD_CHUNK
    nlc, ntc = n_lat // q, t // q

    def chunk(d, s):
        fwd = jnp.where(s < ntc - nlc, nlc + s, s - (ntc - nlc))
        return jnp.where(d == 0, fwd, ntc - 1 - s)

    pad = lambda v: jnp.zeros((2, 1, LANES), F32).at[:, 0, :SSD_HEADS].set(v)
    dsk = jnp.repeat(d_skip, inner // SSD_HEADS)[None, :]
    return pl.pallas_call(
        _ssd_kernel,
        grid=(2, ntc),
        in_specs=[pl.BlockSpec((q, inner), lambda d, s: (chunk(d, s), 0)),
                  pl.BlockSpec((q, gn), lambda d, s: (chunk(d, s), 0)),
                  pl.BlockSpec((q, gn), lambda d, s: (chunk(d, s), 1)),
                  pl.BlockSpec((q, LANES), lambda d, s: (chunk(d, s), d)),
                  pl.BlockSpec((None, 1, LANES), lambda d, s: (d, 0, 0)),
                  pl.BlockSpec((None, 1, LANES), lambda d, s: (d, 0, 0)),
                  pl.BlockSpec((1, inner), lambda d, s: (0, 0))],
        out_specs=pl.BlockSpec((1, q, inner), lambda d, s: (d, chunk(d, s), 0)),
        out_shape=jax.ShapeDtypeStruct((2, t, inner), F32),
        scratch_shapes=[pltpu.VMEM((SSD_HEADS // 2, SSD_STATE, LANES), F32)],
        compiler_params=_cparams("parallel", "arbitrary"),
        name="ssd_scan",
    )(xs, bc, bc, dtp, pad(dt_bias), pad(a_log), dsk)


def _diff_kernel(*refs, stream, nk, lam_init):
    if stream:
        (qt_ref, kc_ref, vct_ref, k_ref, vt_ref, lam_ref, sub_ref, o_ref,
         qs_ref, m_ref, l_ref, acc_ref, s0_ref, s1_ref, mt0_ref, mt1_ref) = refs
    else:
        qt_ref, kc_ref, vct_ref, lam_ref, sub_ref, o_ref, qs_ref, m_ref, l_ref, acc_ref = refs
    j = pl.program_id(2)
    nj = pl.num_programs(2)

    tq = qt_ref.shape[1]
    qc = min(tq, 512)
    chunks = [(i, slice(c, c + qc)) for i in range(2) for c in range(0, tq, qc)]

    def scores(k, s_ref, mt_ref):
        for i, cs in chunks:
            s = _dot(k, qs_ref[i, :, cs])
            s_ref[i, :, cs] = s
            mt_ref[i, :, cs] = jnp.max(s, axis=0, keepdims=True)

    def consume(vt, s_ref, mt_ref):
        vt1 = jnp.concatenate([vt, jnp.ones((ONES_ROWS, vt.shape[1]), MM)], axis=0)
        for i, cs in chunks:
            m_old = m_ref[i, :, cs]
            m_new = jnp.maximum(m_old, mt_ref[i, :, cs])
            alpha = jnp.exp2(m_old - m_new)
            p = jnp.exp2(s_ref[i, :, cs] - m_new)
            r = _dot(vt1, p.astype(MM))
            l_ref[i, :, cs] = alpha * l_ref[i, :, cs] + r[LANES:LANES + 1]
            acc_ref[i, :, cs] = alpha * acc_ref[i, :, cs] + r[:LANES]
            m_ref[i, :, cs] = m_new

    @pl.when(j == 0)
    def _():
        qs_ref[0], qs_ref[1] = _split_heads(qt_ref[...])
        kc = kc_ref[...]
        for i in range(2):
            s = _dot(kc, qs_ref[i])
            m_new = jnp.max(s, axis=0, keepdims=True)
            p = jnp.exp2(s - m_new)
            l_ref[i] = jnp.sum(p, axis=0, keepdims=True)
            acc_ref[i] = _dot(vct_ref[...], p.astype(MM))
            m_ref[i] = m_new
        if stream:
            scores(k_ref[...], s0_ref, mt0_ref)

    if stream:
        bufs = ((s0_ref, mt0_ref), (s1_ref, mt1_ref))
        for par in range(2):
            @pl.when((j > 0) & (j < nk) & (j % 2 == par))
            def _(par=par):
                scores(k_ref[...], *bufs[par])
                consume(vt_ref[...], *bufs[1 - par])

        @pl.when(j == nk)
        def _():
            consume(vt_ref[...], *bufs[(nk - 1) % 2])

    @pl.when(j == nj - 1)
    def _():
        lam4 = lam_ref[...]
        lam = (jnp.exp(jnp.sum(lam4[0:1] * lam4[1:2], axis=1, keepdims=True))
               - jnp.exp(jnp.sum(lam4[2:3] * lam4[3:4], axis=1, keepdims=True)) + lam_init)
        o_t = acc_ref[0] * (1.0 / l_ref[0]) - lam * (acc_ref[1] * (1.0 / l_ref[1]))
        o_ref[...] = _rms(o_t.T, sub_ref[...]) * (1.0 - lam_init)


def _diff_attention(k, qvt, lam4, subln, *, n_lat, lam_init, tq, tk):
    t, width = k.shape
    nc = t - n_lat
    nh = width // LANES
    ctx_blk = n_lat // nc
    k = k.reshape(t, nh, LANES).transpose(1, 0, 2)

    def call(stream, q_rows, q_blk0, tq_, nk):
        in_specs = [pl.BlockSpec((LANES, tq_), lambda h, i, j: (h, q_blk0 + i)),
                    pl.BlockSpec((None, nc, LANES), lambda h, i, j: (h, ctx_blk, 0)),
                    pl.BlockSpec((LANES, nc), lambda h, i, j: (nh + h, ctx_blk))]
        args = [qvt, k, qvt]
        scratch = [pltpu.VMEM((2, LANES, tq_), MM), pltpu.VMEM((2, 1, tq_), F32),
                   pltpu.VMEM((2, 1, tq_), F32), pltpu.VMEM((2, LANES, tq_), F32)]
        if stream:
            in_specs += [pl.BlockSpec((None, tk, LANES), lambda h, i, j: (h, jnp.minimum(j, nk - 1), 0)),
                         pl.BlockSpec((LANES, tk), lambda h, i, j: (nh + h, jnp.maximum(j - 1, 0)))]
            args += [k, qvt]
            scratch += [pltpu.VMEM((2, tk, tq_), F32)] * 2 + [pltpu.VMEM((2, 1, tq_), F32)] * 2
        in_specs += [pl.BlockSpec((4, HEAD), lambda h, i, j: (0, 0)),
                     pl.BlockSpec((1, LANES), lambda h, i, j: (0, 0))]
        args += [lam4, subln]
        return pl.pallas_call(
            functools.partial(_diff_kernel, stream=stream, nk=nk, lam_init=lam_init),
            grid=(nh, q_rows // tq_, nk + 1 if stream else 1),
            in_specs=in_specs,
            out_specs=pl.BlockSpec((tq_, LANES), lambda h, i, j: (i, h)),
            out_shape=jax.ShapeDtypeStruct((q_rows, width), F32),
            scratch_shapes=scratch,
            compiler_params=_cparams("parallel", "parallel", "arbitrary"),
            name="diff_attn" if stream else "diff_attn_ctx",
        )(*args)

    assert n_lat % tq == 0 and n_lat % tk == 0 and n_lat % nc == 0
    return call(True, n_lat, 0, tq, n_lat // tk), call(False, nc, ctx_blk, nc, 0)


def _na_kernel(*refs, local, nb=None):
    if not local:
        qt_ref, kc_ref, vct_ref, o_ref = refs
        outs = []
        for qh in _split_heads(qt_ref[...]):
            s_c = _dot(kc_ref[...], qh)
            p_c = jnp.exp2(s_c - jnp.max(s_c, axis=0, keepdims=True))
            outs.append(_dot(vct_ref[...], p_c.astype(MM)) * (1.0 / jnp.sum(p_c, axis=0, keepdims=True)))
        top = lax.broadcasted_iota(jnp.int32, outs[0].shape, 0) < HEAD
        o_ref[...] = jnp.where(top, outs[0], outs[1]).T
        return

    (qt_ref, kw_ref, vwt_ref, kc_ref, vct_ref, bias_ref, o_ref,
     sl0_ref, sl1_ref, sc0_ref, sc1_ref, m0_ref, m1_ref) = refs
    b = pl.program_id(1)
    bufs = ((sl0_ref, sc0_ref, m0_ref), (sl1_ref, sc1_ref, m1_ref))

    def scores(sl_ref, sc_ref, m_ref):
        for half, qh in enumerate(_split_heads(qt_ref[...])):
            s_c = _dot(kc_ref[...], qh)
            s_l = _dot(kw_ref[...], qh) + bias_ref[half]
            sc_ref[half] = s_c
            sl_ref[half] = s_l
            m_ref[half] = jnp.maximum(jnp.max(s_c, axis=0, keepdims=True), jnp.max(s_l, axis=0, keepdims=True))

    def consume(sl_ref, sc_ref, m_ref):
        outs = []
        for half in range(2):
            m = m_ref[half]
            p_c = jnp.exp2(sc_ref[half] - m)
            p_l = jnp.exp2(sl_ref[half] - m)
            l = jnp.sum(p_c, axis=0, keepdims=True) + jnp.sum(p_l, axis=0, keepdims=True)
            o_t = _dot(vct_ref[...], p_c.astype(MM)) + _dot(vwt_ref[...], p_l.astype(MM))
            outs.append(o_t * (1.0 / l))
        top = lax.broadcasted_iota(jnp.int32, outs[0].shape, 0) < HEAD
        o_ref[...] = jnp.where(top, outs[0], outs[1]).T

    @pl.when(b == 0)
    def _():
        scores(*bufs[0])

    for par in range(2):
        @pl.when((b > 0) & (b < nb) & (b % 2 == par))
        def _(par=par):
            scores(*bufs[par])
            consume(*bufs[1 - par])

    @pl.when(b == nb)
    def _():
        consume(*bufs[(nb - 1) % 2])


def _na_geometry(rows):
    kr = min(NA_KR, rows)
    assert kr == NA_KR and rows % NA_RB == 0 and rows >= NA_WIN and (rows - NA_WIN) % 2 == 0
    nb = rows // NA_RB
    ws = np.clip(NA_RB * np.arange(nb) - kr // 2, 0, rows - NA_WIN)

    def rowpart(b):
        r = NA_RB * b + np.arange(NA_RB)
        r0 = np.clip(r - kr // 2, 0, rows - kr)
        key = ws[b] + np.arange(NA_WIN)
        valid = (key[:, None] >= r0[None, :]) & (key[:, None] < r0[None, :] + kr)
        brow = np.clip(key[:, None] - r[None, :] + (NA_KR - 1), 0, 2 * NA_KR - 2)
        return valid, brow

    variant_blocks = [0, min(1, nb - 1), nb - 1]
    variant_of = np.where(np.arange(nb) == 0, 0, np.where(np.arange(nb) == nb - 1, 2, 1))
    for b in range(nb):
        vb, bb = rowpart(b)
        vv, bv = rowpart(variant_blocks[variant_of[b]])
        assert (vb == vv).all() and (np.where(vb, bb, 0) == np.where(vv, bv, 0)).all()
    return nb, ws, [rowpart(b) for b in variant_blocks]


def _na_bias_table(rpb, rows):
    _, _, parts = _na_geometry(rows)
    kw = min(NA_KC, GRID_W)
    w = np.arange(GRID_W)
    c0 = np.clip(w - kw // 2, 0, GRID_W - kw)
    kc = np.arange(GRID_W)
    cvalid = (kc[:, None] >= c0[None, :]) & (kc[:, None] < c0[None, :] + kw)
    bcol = np.clip(kc[:, None] - w[None, :] + (NA_KC - 1), 0, 2 * NA_KC - 2)
    onehot = jnp.asarray(bcol[:, :, None] == np.arange(2 * NA_KC - 1)[None, None, :], F32)
    tabs = []
    for rvalid, brow in parts:
        rows_sel = rpb[:, brow]
        tab = jnp.einsum('hkrb,cwb->hkcrw', rows_sel, onehot, precision=lax.Precision.HIGHEST)
        valid = rvalid[:, None, :, None] & cvalid[None, :, None, :]
        tab = jnp.where(valid[None], tab * LOG2E, NEG)
        tabs.append(tab.reshape(rpb.shape[0], NA_WIN * GRID_W, NA_RB * GRID_W))
    return jnp.stack(tabs).astype(F32)


def _na_attention(k, qvt, rpb, *, n_lat):
    t, width = k.shape
    nc = t - n_lat
    nh = width // LANES
    rows = n_lat // GRID_W
    ctx_blk = n_lat // nc
    nb, ws_np, _ = _na_geometry(rows)
    table = _na_bias_table(rpb, rows)
    nq, win = NA_RB * GRID_W, NA_WIN * GRID_W
    assert nc % nq == 0

    def ws(b):
        return jnp.clip(NA_RB // 2 * b - NA_KR // 4, 0, (rows - NA_WIN) // 2) * (2 * GRID_W)

    def variant(b):
        return jnp.where(b == 0, 0, jnp.where(b == nb - 1, 2, 1))

    assert (ws_np == np.clip(NA_RB * np.arange(nb) - NA_KR // 2, 0, rows - NA_WIN)).all()
    cur = lambda b: jnp.minimum(b, nb - 1)
    prev = lambda b: jnp.maximum(b - 1, 0)
    o_lat = pl.pallas_call(
        functools.partial(_na_kernel, local=True, nb=nb),
        grid=(nh, nb + 1),
        in_specs=[pl.BlockSpec((LANES, nq), lambda p, b: (p, cur(b))),
                  pl.BlockSpec((pl.Element(win), pl.Element(LANES)), lambda p, b: (ws(cur(b)), p * LANES)),
                  pl.BlockSpec((pl.Element(LANES), pl.Element(win)), lambda p, b: ((nh + p) * LANES, ws(prev(b)))),
                  pl.BlockSpec((nc, LANES), lambda p, b: (ctx_blk, p)),
                  pl.BlockSpec((LANES, nc), lambda p, b: (nh + p, ctx_blk)),
                  pl.BlockSpec((None, 2, win, nq), lambda p, b: (variant(cur(b)), p, 0, 0))],
        out_specs=pl.BlockSpec((nq, LANES), lambda p, b: (prev(b), p)),
        out_shape=jax.ShapeDtypeStruct((n_lat, width), F32),
        scratch_shapes=[pltpu.VMEM((2, win, nq), F32)] * 2 + [pltpu.VMEM((2, nc, nq), F32)] * 2
                       + [pltpu.VMEM((2, 1, nq), F32)] * 2,
        compiler_params=_cparams("parallel", "arbitrary"),
        name="na_attn",
    )(qvt, k, qvt, k, qvt, table)
    o_ctx = pl.pallas_call(
        functools.partial(_na_kernel, local=False),
        grid=(nh,),
        in_specs=[pl.BlockSpec((LANES, nc), lambda p: (p, ctx_blk)),
                  pl.BlockSpec((nc, LANES), lambda p: (ctx_blk, p)),
                  pl.BlockSpec((LANES, nc), lambda p: (nh + p, ctx_blk))],
        out_specs=pl.BlockSpec((nc, LANES), lambda p: (0, p)),
        out_shape=jax.ShapeDtypeStruct((nc, width), F32),
        compiler_params=_cparams("parallel"),
        name="na_attn_ctx",
    )(qvt, k, qvt)
    return o_lat, o_ctx


def _rope_tables(n_lat, n_ctx):
    tok = jnp.arange(n_lat)
    row = (tok // GRID_W).astype(F32)
    col = (tok % GRID_W).astype(F32)
    n_freq = HEAD // 4
    inv = ROPE_BASE ** (-jnp.arange(n_freq, dtype=F32) / n_freq)
    ang = jnp.concatenate([row[:, None] * inv, col[:, None] * inv], axis=-1)
    cos, sin = jnp.cos(ang), jnp.sin(ang)
    cos = jnp.concatenate([jnp.tile(cos, (1, 4)), jnp.ones((n_ctx, LANES), F32)], axis=0)
    sin = jnp.concatenate([jnp.concatenate([-sin, sin, -sin, sin], axis=1), jnp.zeros((n_ctx, LANES), F32)], axis=0)
    return cos, sin


def _pick_tm(t, cap):
    return max(m for m in range(8, cap + 1, 8) if t % m == 0)


def kernel(x, c, ctx, c_ctx, ada_w, ada_b, norm_pre, norm_post, ssd_w_in, ssd_conv_w, ssd_conv_b, ssd_dt_bias, ssd_a_log, ssd_d, ssd_norm, ssd_w_out, dif_w_in, dif_lam_q1, dif_lam_k1, dif_lam_q2, dif_lam_k2, dif_subln, dif_w_out, na_w_in, na_rpb, na_w_out):
    n_lat, d = x.shape[1], x.shape[2]
    n_ctx = ctx.shape[1]
    depth = ada_w.shape[0]
    xa = jnp.concatenate([x[0], ctx[0]], axis=0)
    t = xa.shape[0]
    mods = _modulation(c, c_ctx, ada_w, ada_b)
    rope = None
    tm_in = _pick_tm(t, 1280)
    tm_sum = _pick_tm(t, 640)
    tm_lat = _pick_tm(n_lat, 512)
    q_scale = HEAD ** -0.5 * LOG2E
    ia = ib = ic = 0
    for i in range(depth):
        last = i == depth - 1
        out_rows = n_lat if last else t
        shift, scale, gate = (mods[i, 0:2, k * d:(k + 1) * d] for k in range(3))
        nw = norm_pre[i][None, :]
        pw = norm_post[i][None, :]
        inproj = functools.partial(_inproj, xa, nw, scale, shift, n_lat=n_lat, tm=tm_in)
        kind = i % 3
        if kind == 0:
            w_in = ssd_w_in[ia]
            inner = ssd_w_out.shape[1]
            gn2 = w_in.shape[1] - 2 * inner - 2 * SSD_HEADS
            u = inproj(w_in[:, :2 * inner + gn2].astype(MM), out_dtype=F32, tn=TN_WIDE, name="ssd_inproj")
            w_dt = jnp.zeros((d, 2 * LANES), F32)
            w_dt = w_dt.at[:, :SSD_HEADS].set(w_in[:, 2 * inner + gn2:2 * inner + gn2 + SSD_HEADS])
            w_dt = w_dt.at[:, LANES:LANES + SSD_HEADS].set(w_in[:, 2 * inner + gn2 + SSD_HEADS:])
            dtp = inproj(w_dt.astype(MM), out_dtype=F32, tn=2 * LANES, name="ssd_inproj_dt")
            cw, cb = ssd_conv_w[ia], ssd_conv_b[ia][None, :]
            xs = _conv_silu(u, inner, inner, cw[:, :inner], cb[:, :inner], n_lat=n_lat, tm=n_ctx, tc=1024,
                            out_dtype=F32, name="ssd_conv_x")
            bc = _conv_silu(u, 2 * inner, gn2, cw[:, inner:], cb[:, inner:], n_lat=n_lat, tm=n_ctx, tc=1024,
                            out_dtype=MM, name="ssd_conv_bc")
            y = _ssd_scan(xs, bc, dtp, ssd_dt_bias[ia], ssd_a_log[ia], ssd_d[ia], n_lat=n_lat)
            tm = tm_lat if last else tm_sum
            y_ops = [(y, pl.BlockSpec((None, tm, inner), lambda r, dd=dd: (dd, r, 0))) for dd in range(2)]
            z_op = (u, pl.BlockSpec((tm, inner), lambda r: (r, 0)))
            xa = _outproj(y_ops, "sum", z_op, ssd_norm[ia][None, :], ssd_w_out[ia].astype(MM), xa, pw, gate,
                          n_lat=n_lat, tm=tm, out_rows=out_rows, name="ssd_outproj")
            ia += 1
        else:
            if kind == 1:
                w_in, w_out, width = dif_w_in[ib], dif_w_out[ib], dif_w_out.shape[1]
                if rope is None:
                    rope = _rope_tables(n_lat, n_ctx)
                rope_k, rope_q, n_rope = rope, tuple(tab.T for tab in rope), width // TN_QKV
            else:
                w_in, w_out, width = na_w_in[ic], na_w_out[ic], d
                rope_k = rope_q = None
                n_rope = 0
            k = inproj(w_in[:, width:2 * width].astype(MM), out_dtype=MM, tn=TN_QKV, n_rope=n_rope, rope=rope_k,
                       name="attn_inproj_k")
            w_qv = jnp.concatenate([w_in[:, :width], w_in[:, 2 * width:3 * width]], axis=1).T.astype(MM)
            qvt = inproj(w_qv, out_dtype=MM, tn=TN_QKV, n_q=width // TN_QKV, n_rope=n_rope, q_scale=q_scale,
                         rope=rope_q, transposed=True, name="attn_inproj_qvt")
            g = inproj(w_in[:, 3 * width:].astype(MM), out_dtype=F32, tn=TN_WIDE, name="attn_inproj_g")
            if kind == 1:
                lam_init = 0.8 - 0.6 * math.exp(-0.3 * i)
                lam4 = jnp.stack([dif_lam_q1[ib], dif_lam_k1[ib], dif_lam_q2[ib], dif_lam_k2[ib]])
                o_lat, o_ctx = _diff_attention(k, qvt, lam4, dif_subln[ib][None, :], n_lat=n_lat,
                                               lam_init=lam_init, tq=min(1024, n_lat), tk=min(2048, n_lat))
                ib += 1
            else:
                o_lat, o_ctx = _na_attention(k, qvt, na_rpb[ic], n_lat=n_lat)
                ic += 1
            g_op = (g, pl.BlockSpec((n_ctx, width), lambda r: (r, 0)))
            xa = _outproj(_attn_out_ops(o_lat, o_ctx, n_ctx), "select", g_op, None, w_out.astype(MM), xa, pw,
                          gate, n_lat=n_lat, tm=n_ctx, out_rows=out_rows, name="attn_outproj")
    return xa[:n_lat][None]
```

```python
import functools
import math

import numpy as np
import jax
import jax.numpy as jnp
from jax import lax
from jax.experimental import pallas as pl
from jax.experimental.pallas import tpu as pltpu

F32 = jnp.float32
MM = jnp.bfloat16

EPS = 1e-6
GRID_W = 64
LANES = 128
HEAD = 64
NEG = -1e30
LOG2E = math.log2(math.e)

SSD_HEADS = 32
SSD_STATE = 128
SSD_CONV = 5
SSD_CHUNK = 128

ONES_ROWS = 16

NA_KR = 8
NA_KC = 16
NA_RB = 4
NA_WIN = NA_RB + NA_KR
ROPE_BASE = 10000.0

VMEM_LIMIT = 52 * 1024 * 1024
TN_WIDE = 1024
TN_QKV = 512


def _cparams(*sem):
    return pltpu.CompilerParams(dimension_semantics=sem, vmem_limit_bytes=VMEM_LIMIT)


def _silu(v):
    return v * (1.0 / (1.0 + jnp.exp(-v)))


def _softplus(v):
    return jnp.maximum(v, 0.0) + jnp.log(1.0 + jnp.exp(-jnp.abs(v)))


def _rms(v, w):
    return v * lax.rsqrt(jnp.mean(v * v, axis=-1, keepdims=True) + EPS) * w


def _dot(a, b):
    return jnp.dot(a, b, preferred_element_type=F32)


def _dot_nt(a, b):
    return lax.dot_general(a, b, (((1,), (1,)), ((), ())), preferred_element_type=F32)


def _lane_lo(shape):
    return lax.broadcasted_iota(jnp.int32, shape, len(shape) - 1) % LANES < HEAD


def _split_heads(qt):
    top = lax.broadcasted_iota(jnp.int32, qt.shape, 0) < HEAD
    zero = jnp.zeros_like(qt)
    return jnp.where(top, qt, zero), jnp.where(top, zero, qt)


def _mod_kernel(cc_ref, w_ref, b_ref, o_ref):
    s = _silu(cc_ref[...])
    o_ref[0] = _dot(s.astype(MM), w_ref[0].astype(MM)) + b_ref[0]


def _modulation(c, c_ctx, ada_w, ada_b):
    depth, d, d3 = ada_w.shape
    cc = jnp.zeros((8, d), F32).at[0].set(c[0]).at[1].set(c_ctx)
    out = pl.pallas_call(
        _mod_kernel,
        grid=(depth, d3 // d),
        in_specs=[pl.BlockSpec((8, d), lambda i, j: (0, 0)),
                  pl.BlockSpec((1, d, d), lambda i, j: (i, 0, j)),
                  pl.BlockSpec((1, 1, d), lambda i, j: (i, 0, j))],
        out_specs=pl.BlockSpec((1, 8, d), lambda i, j: (i, 0, j)),
        out_shape=jax.ShapeDtypeStruct((depth, 8, d3), F32),
        compiler_params=_cparams("parallel", "parallel"),
        name="adaln_mod",
    )(cc, ada_w, ada_b.reshape(depth, 1, d3))
    return out


def _row_select(i, tm, n_lat, v2):
    rows = i * tm + lax.broadcasted_iota(jnp.int32, (tm, 1), 0)
    return jnp.where(rows < n_lat, v2[0:1, :], v2[1:2, :])


def _inproj_kernel(*refs, tm, n_lat, n_q, n_rope, q_scale, transposed):
    if n_rope:
        x_ref, nw_ref, sc_ref, sh_ref, w_ref, cos_ref, sin_ref, o_ref, h_ref = refs
    else:
        x_ref, nw_ref, sc_ref, sh_ref, w_ref, o_ref, h_ref = refs
    i, j = pl.program_id(0), pl.program_id(1)
    fa = 0 if transposed else 1

    @pl.when(j == 0)
    def _():
        h = _rms(x_ref[...], nw_ref[...])
        h = h * (1.0 + _row_select(i, tm, n_lat, sc_ref[...])) + _row_select(i, tm, n_lat, sh_ref[...])
        h_ref[...] = h.astype(MM)

    acc = _dot_nt(w_ref[...], h_ref[...]) if transposed else _dot(h_ref[...], w_ref[...])
    if n_q:
        acc = acc * jnp.where(j < n_q, q_scale, 1.0)
    if n_rope:
        tn = acc.shape[fa]

        @pl.when(j < n_rope)
        def _():
            reps = tn // LANES
            cos = jnp.concatenate([cos_ref[...]] * reps, axis=fa)
            sin = jnp.concatenate([sin_ref[...]] * reps, axis=fa)
            feat = lax.broadcasted_iota(jnp.int32, acc.shape, fa)
            swapped = jnp.where(feat % HEAD < HEAD // 2,
                                pltpu.roll(acc, tn - HEAD // 2, fa), pltpu.roll(acc, HEAD // 2, fa))
            o_ref[...] = (acc * cos + swapped * sin).astype(o_ref.dtype)

        @pl.when(j >= n_rope)
        def _():
            o_ref[...] = acc.astype(o_ref.dtype)
    else:
        o_ref[...] = acc.astype(o_ref.dtype)


def _inproj(xa, nw, sc, sh, w, *, n_lat, out_dtype, tm, tn, n_q=0, n_rope=0, q_scale=1.0, rope=None,
            transposed=False, name="inproj"):
    t, d = xa.shape
    f = w.shape[0] if transposed else w.shape[1]
    assert t % tm == 0 and f % tn == 0
    in_specs = [pl.BlockSpec((tm, d), lambda i, j: (i, 0)),
                pl.BlockSpec((1, d), lambda i, j: (0, 0)),
                pl.BlockSpec((2, d), lambda i, j: (0, 0)),
                pl.BlockSpec((2, d), lambda i, j: (0, 0)),
                pl.BlockSpec((tn, d), lambda i, j: (j, 0)) if transposed else pl.BlockSpec((d, tn), lambda i, j: (0, j))]
    args = [xa, nw, sc, sh, w]
    if n_rope:
        tab = pl.BlockSpec((LANES, tm), lambda i, j: (0, i)) if transposed else pl.BlockSpec((tm, LANES), lambda i, j: (i, 0))
        in_specs += [tab, tab]
        args += list(rope)
    if transposed:
        out_spec, out_shape = pl.BlockSpec((tn, tm), lambda i, j: (j, i)), (f, t)
    else:
        out_spec, out_shape = pl.BlockSpec((tm, tn), lambda i, j: (i, j)), (t, f)
    return pl.pallas_call(
        functools.partial(_inproj_kernel, tm=tm, n_lat=n_lat, n_q=n_q, n_rope=n_rope, q_scale=q_scale,
                          transposed=transposed),
        grid=(t // tm, f // tn),
        in_specs=in_specs,
        out_specs=out_spec,
        out_shape=jax.ShapeDtypeStruct(out_shape, out_dtype),
        scratch_shapes=[pltpu.VMEM((tm, d), MM)],
        compiler_params=_cparams("parallel", "arbitrary"),
        name=name,
    )(*args)


def _outproj_kernel(*refs, tm, n_lat, n_a, combine, use_norm):
    a_refs = refs[:n_a]
    g_ref = refs[n_a]
    rest = refs[n_a + 1:]
    if use_norm:
        nm_ref, rest = rest[0], rest[1:]
    w_ref, x_ref, pw_ref, gate_ref, o_ref = rest
    i = pl.program_id(0)
    if combine == "sum":
        a = a_refs[0][...].astype(F32)
        for r in a_refs[1:]:
            a = a + r[...].astype(F32)
    else:
        a = jnp.where(i * tm < n_lat, a_refs[0][...], a_refs[1][...]).astype(F32)
    a = a * _silu(g_ref[...])
    if use_norm:
        a = _rms(a, nm_ref[...])
    o = _dot(a.astype(MM), w_ref[...])
    o = _rms(o, pw_ref[...])
    o_ref[...] = x_ref[...] + _row_select(i, tm, n_lat, gate_ref[...]) * o


def _outproj(a_ops, combine, g_op, norm_w, w, xa, pw, gate, *, n_lat, tm, out_rows, name="outproj"):
    t, d = xa.shape
    f = w.shape[0]
    assert out_rows % tm == 0 and (n_lat % tm == 0 or combine == "sum")
    tok = pl.BlockSpec((tm, d), lambda i: (i, 0))
    full = lambda r, c: pl.BlockSpec((r, c), lambda i: (0, 0))
    in_specs = [spec for _, spec in a_ops] + [g_op[1]]
    args = [arr for arr, _ in a_ops] + [g_op[0]]
    if norm_w is not None:
        in_specs.append(full(1, f))
        args.append(norm_w)
    in_specs += [full(f, d), tok, full(1, d), full(2, d)]
    args += [w, xa, pw, gate]
    return pl.pallas_call(
        functools.partial(_outproj_kernel, tm=tm, n_lat=n_lat, n_a=len(a_ops), combine=combine,
                          use_norm=norm_w is not None),
        grid=(out_rows // tm,),
        in_specs=in_specs,
        out_specs=tok,
        out_shape=jax.ShapeDtypeStruct((out_rows, d), F32),
        compiler_params=_cparams("parallel"),
        name=name,
    )(*args)


def _attn_out_ops(o_lat, o_ctx, tm):
    nl = o_lat.shape[0] // tm
    f = o_lat.shape[1]
    return [(o_lat, pl.BlockSpec((tm, f), lambda i: (jnp.minimum(i, nl - 1), 0))),
            (o_ctx, pl.BlockSpec((tm, f), lambda i: (jnp.maximum(i - nl, 0), 0)))]


def _conv_kernel(prev_ref, cur_ref, next_ref, w_ref, b_ref, o_ref, *, tm, first_blocks, last_blocks):
    i = pl.program_id(0)
    is_first = functools.reduce(jnp.logical_or, [i == b for b in first_blocks])
    is_last = functools.reduce(jnp.logical_or, [i == b for b in last_blocks])
    u = jnp.concatenate([jnp.where(is_first, 0.0, prev_ref[...]), cur_ref[...],
                         jnp.where(is_last, 0.0, next_ref[...])], axis=0)
    rows = u.shape[0]
    tap = [w_ref[k:k + 1, :] * u for k in range(SSD_CONV)]
    later = pltpu.roll(tap[3] + pltpu.roll(tap[4], rows - 1, 0), rows - 1, 0)
    earlier = pltpu.roll(tap[1] + pltpu.roll(tap[0], 1, 0), 1, 0)
    acc = (tap[2] + later + earlier)[8:8 + tm] + b_ref[...]
    o_ref[...] = _silu(acc).astype(o_ref.dtype)


def _conv_silu(u, col0, width, conv_w, conv_b, *, n_lat, tm, tc, out_dtype, name):
    assert SSD_CONV == 5
    t = u.shape[0]
    assert t % tm == 0 and n_lat % tm == 0 and width % tc == 0 and col0 % tc == 0 and tm % 8 == 0
    cb0 = col0 // tc
    r8 = tm // 8
    nb8 = t // 8
    nl, nt = n_lat // tm, t // tm
    return pl.pallas_call(
        functools.partial(_conv_kernel, tm=tm, first_blocks=(0, nl), last_blocks=(nl - 1, nt - 1)),
        grid=(nt, width // tc),
        in_specs=[pl.BlockSpec((8, tc), lambda i, j: (jnp.maximum(i * r8 - 1, 0), cb0 + j)),
                  pl.BlockSpec((tm, tc), lambda i, j: (i, cb0 + j)),
                  pl.BlockSpec((8, tc), lambda i, j: (jnp.minimum((i + 1) * r8, nb8 - 1), cb0 + j)),
                  pl.BlockSpec((SSD_CONV, tc), lambda i, j: (0, j)),
                  pl.BlockSpec((1, tc), lambda i, j: (0, j))],
        out_specs=pl.BlockSpec((tm, tc), lambda i, j: (i, j)),
        out_shape=jax.ShapeDtypeStruct((t, width), out_dtype),
        compiler_params=_cparams("parallel", "parallel"),
        name=name,
    )(u, u, u, conv_w, conv_b)


def _ssd_kernel(x_ref, b_ref, c_ref, dt_ref, bias_ref, alog_ref, dskip_ref, y_ref, h_ref):
    d, s = pl.program_id(0), pl.program_id(1)
    q = SSD_CHUNK

    @pl.when(s == 0)
    def _():
        h_ref[...] = jnp.zeros(h_ref.shape, F32)

    dt = _softplus(dt_ref[...] + bias_ref[...])
    a = dt * (-jnp.exp(alog_ref[...]))
    ii = lax.broadcasted_iota(jnp.int32, (q, q), 0)
    jj = lax.broadcasted_iota(jnp.int32, (q, q), 1)
    keep = jnp.where(d == 0, jj - ii, ii - jj) <= 0
    cs = jnp.dot(keep.astype(F32), a, preferred_element_type=F32, precision=lax.Precision.HIGHEST)
    cs_t = cs.T
    dt_t = dt.T
    tot = jnp.sum(a, axis=0, keepdims=True)
    tot_t = jnp.sum(a.T, axis=1, keepdims=True)
    w_t = jnp.exp(tot_t - cs_t) * dt_t
    dec = jnp.exp(tot)
    lane_lo = _lane_lo((q, LANES))
    skip_on = jnp.where(d == 0, 1.0, 0.0)

    n_groups = b_ref.shape[1] // SSD_STATE
    hpg = SSD_HEADS // n_groups
    for g in range(n_groups):
        bg = b_ref[:, g * SSD_STATE:(g + 1) * SSD_STATE]
        cg = c_ref[:, g * SSD_STATE:(g + 1) * SSD_STATE]
        cb = _dot_nt(cg, bg)
        bg_t = bg.astype(F32).T
        cg32 = cg.astype(F32)
        for pr in range(hpg // 2):
            p = g * (hpg // 2) + pr
            x32 = x_ref[:, p * LANES:(p + 1) * LANES]
            xb = x32.astype(MM)
            hp = h_ref[p]
            lhs_y, lhs_h = [], []
            for half in range(2):
                hd = 2 * p + half
                col = jnp.broadcast_to(cs[:, hd:hd + 1], (q, q))
                row = jnp.broadcast_to(cs_t[hd:hd + 1, :], (q, q))
                lmat = jnp.exp(jnp.where(keep, col - row, NEG))
                m = cb * lmat * dt_t[hd:hd + 1, :]
                ce = cg32 * jnp.exp(col)
                lhs_y.append(jnp.concatenate([m.astype(MM), ce.astype(MM)], axis=1))
                lhs_h.append((bg_t * w_t[hd:hd + 1, :]).astype(MM))
            rhs = jnp.concatenate([xb, hp.astype(MM)], axis=0)
            yf = _dot(jnp.concatenate(lhs_y, axis=0), rhs)
            hf = _dot(jnp.concatenate(lhs_h, axis=0), xb)
            y = jnp.where(lane_lo, yf[:q], yf[q:])
            y_ref[0, :, p * LANES:(p + 1) * LANES] = (
                y + skip_on * dskip_ref[:, p * LANES:(p + 1) * LANES] * x32).astype(y_ref.dtype)
            hn = jnp.where(lane_lo, hf[:SSD_STATE], hf[SSD_STATE:])
            hd0 = 2 * p
            dec_p = jnp.where(lane_lo[0:1, :], dec[:, hd0:hd0 + 1], dec[:, hd0 + 1:hd0 + 2])
            h_ref[p] = hp * dec_p + hn


def _ssd_scan(xs, bc, dtp, dt_blk, dt_bias, a_log, d_skip, *, n_lat):
    t, inner = xs.shape
    gn = bc.shape[1] // 2
    q = SSD_CHUNK
    nlc, ntc = n_lat // q, t // q

    def chunk(d, s):
        fwd = jnp.where(s < ntc - nlc, nlc + s, s - (ntc - nlc))
        return jnp.where(d == 0, fwd, ntc - 1 - s)

    pad = lambda v: jnp.zeros((2, 1, LANES), F32).at[:, 0, :SSD_HEADS].set(v)
    dsk = jnp.repeat(d_skip, inner // SSD_HEADS)[None, :]
    return pl.pallas_call(
        _ssd_kernel,
        grid=(2, ntc),
        in_specs=[pl.BlockSpec((q, inner), lambda d, s: (chunk(d, s), 0)),
                  pl.BlockSpec((q, gn), lambda d, s: (chunk(d, s), 0)),
                  pl.BlockSpec((q, gn), lambda d, s: (chunk(d, s), 1)),
                  pl.BlockSpec((q, LANES), lambda d, s: (chunk(d, s), dt_blk + d)),
                  pl.BlockSpec((None, 1, LANES), lambda d, s: (d, 0, 0)),
                  pl.BlockSpec((None, 1, LANES), lambda d, s: (d, 0, 0)),
                  pl.BlockSpec((1, inner), lambda d, s: (0, 0))],
        out_specs=pl.BlockSpec((1, q, inner), lambda d, s: (d, chunk(d, s), 0)),
        out_shape=jax.ShapeDtypeStruct((2, t, inner), MM),
        scratch_shapes=[pltpu.VMEM((SSD_HEADS // 2, SSD_STATE, LANES), F32)],
        compiler_params=_cparams("parallel", "arbitrary"),
        name="ssd_scan",
    )(xs, bc, bc, dtp, pad(dt_bias), pad(a_log), dsk)


def _diff_kernel(*refs, stream, nk, lam_init):
    if stream:
        (qt_ref, kc_ref, vct_ref, k_ref, vt_ref, lam_ref, sub_ref, o_ref,
         qs_ref, m_ref, l_ref, acc_ref, s0_ref, s1_ref, mt0_ref, mt1_ref) = refs
    else:
        qt_ref, kc_ref, vct_ref, lam_ref, sub_ref, o_ref, qs_ref, m_ref, l_ref, acc_ref = refs
    j = pl.program_id(2)
    nj = pl.num_programs(2)

    tq = qt_ref.shape[1]
    qc = min(tq, 512)
    chunks = [(i, slice(c, c + qc)) for i in range(2) for c in range(0, tq, qc)]

    def scores(k, s_ref, mt_ref):
        for i, cs in chunks:
            s = _dot(k, qs_ref[i, :, cs])
            s_ref[i, :, cs] = s
            mt_ref[i, :, cs] = jnp.max(s, axis=0, keepdims=True)

    def consume(vt, s_ref, mt_ref):
        vt1 = jnp.concatenate([vt, jnp.ones((ONES_ROWS, vt.shape[1]), MM)], axis=0)
        for i, cs in chunks:
            m_old = m_ref[i, :, cs]
            m_new = jnp.maximum(m_old, mt_ref[i, :, cs])
            alpha = jnp.exp2(m_old - m_new)
            p = jnp.exp2(s_ref[i, :, cs] - m_new)
            r = _dot(vt1, p.astype(MM))
            l_ref[i, :, cs] = alpha * l_ref[i, :, cs] + r[LANES:LANES + 1]
            acc_ref[i, :, cs] = alpha * acc_ref[i, :, cs] + r[:LANES]
            m_ref[i, :, cs] = m_new

    @pl.when(j == 0)
    def _():
        qs_ref[0], qs_ref[1] = _split_heads(qt_ref[...])
        kc = kc_ref[...]
        for i in range(2):
            s = _dot(kc, qs_ref[i])
            m_new = jnp.max(s, axis=0, keepdims=True)
            p = jnp.exp2(s - m_new)
            l_ref[i] = jnp.sum(p, axis=0, keepdims=True)
            acc_ref[i] = _dot(vct_ref[...], p.astype(MM))
            m_ref[i] = m_new
        if stream:
            scores(k_ref[...], s0_ref, mt0_ref)

    if stream:
        bufs = ((s0_ref, mt0_ref), (s1_ref, mt1_ref))
        for par in range(2):
            @pl.when((j > 0) & (j < nk) & (j % 2 == par))
            def _(par=par):
                scores(k_ref[...], *bufs[par])
                consume(vt_ref[...], *bufs[1 - par])

        @pl.when(j == nk)
        def _():
            consume(vt_ref[...], *bufs[(nk - 1) % 2])

    @pl.when(j == nj - 1)
    def _():
        lam4 = lam_ref[...]
        lam = (jnp.exp(jnp.sum(lam4[0:1] * lam4[1:2], axis=1, keepdims=True))
               - jnp.exp(jnp.sum(lam4[2:3] * lam4[3:4], axis=1, keepdims=True)) + lam_init)
        o_t = acc_ref[0] * (1.0 / l_ref[0]) - lam * (acc_ref[1] * (1.0 / l_ref[1]))
        o_ref[...] = (_rms(o_t.T, sub_ref[...]) * (1.0 - lam_init)).astype(o_ref.dtype)


def _diff_attention(k, qvt, lam4, subln, *, n_lat, lam_init, tq, tk):
    t, width = k.shape
    nc = t - n_lat
    nh = width // LANES
    ctx_blk = n_lat // nc
    k = k.reshape(t, nh, LANES).transpose(1, 0, 2)

    def call(stream, q_rows, q_blk0, tq_, nk):
        in_specs = [pl.BlockSpec((LANES, tq_), lambda h, i, j: (h, q_blk0 + i)),
                    pl.BlockSpec((None, nc, LANES), lambda h, i, j: (h, ctx_blk, 0)),
                    pl.BlockSpec((LANES, nc), lambda h, i, j: (nh + h, ctx_blk))]
        args = [qvt, k, qvt]
        scratch = [pltpu.VMEM((2, LANES, tq_), MM), pltpu.VMEM((2, 1, tq_), F32),
                   pltpu.VMEM((2, 1, tq_), F32), pltpu.VMEM((2, LANES, tq_), F32)]
        if stream:
            in_specs += [pl.BlockSpec((None, tk, LANES), lambda h, i, j: (h, jnp.minimum(j, nk - 1), 0)),
                         pl.BlockSpec((LANES, tk), lambda h, i, j: (nh + h, jnp.maximum(j - 1, 0)))]
            args += [k, qvt]
            scratch += [pltpu.VMEM((2, tk, tq_), F32)] * 2 + [pltpu.VMEM((2, 1, tq_), F32)] * 2
        in_specs += [pl.BlockSpec((4, HEAD), lambda h, i, j: (0, 0)),
                     pl.BlockSpec((1, LANES), lambda h, i, j: (0, 0))]
        args += [lam4, subln]
        return pl.pallas_call(
            functools.partial(_diff_kernel, stream=stream, nk=nk, lam_init=lam_init),
            grid=(nh, q_rows // tq_, nk + 1 if stream else 1),
            in_specs=in_specs,
            out_specs=pl.BlockSpec((tq_, LANES), lambda h, i, j: (i, h)),
            out_shape=jax.ShapeDtypeStruct((q_rows, width), MM),
            scratch_shapes=scratch,
            compiler_params=_cparams("parallel", "parallel", "arbitrary"),
            name="diff_attn" if stream else "diff_attn_ctx",
        )(*args)

    assert n_lat % tq == 0 and n_lat % tk == 0 and n_lat % nc == 0
    return call(True, n_lat, 0, tq, n_lat // tk), call(False, nc, ctx_blk, nc, 0)


def _na_kernel(*refs, local, nb=None):
    if not local:
        qt_ref, kc_ref, vct_ref, o_ref = refs
        outs = []
        for qh in _split_heads(qt_ref[...]):
            s_c = _dot(kc_ref[...], qh)
            p_c = jnp.exp2(s_c - jnp.max(s_c, axis=0, keepdims=True))
            outs.append(_dot(vct_ref[...], p_c.astype(MM)) * (1.0 / jnp.sum(p_c, axis=0, keepdims=True)))
        top = lax.broadcasted_iota(jnp.int32, outs[0].shape, 0) < HEAD
        o_ref[...] = jnp.where(top, outs[0], outs[1]).T.astype(o_ref.dtype)
        return

    (qt_ref, kw_ref, vwt_ref, kc_ref, vct_ref, bias_ref, o_ref,
     sl0_ref, sl1_ref, sc0_ref, sc1_ref, m0_ref, m1_ref) = refs
    b = pl.program_id(1)
    bufs = ((sl0_ref, sc0_ref, m0_ref), (sl1_ref, sc1_ref, m1_ref))

    def scores(sl_ref, sc_ref, m_ref):
        for half, qh in enumerate(_split_heads(qt_ref[...])):
            s_c = _dot(kc_ref[...], qh)
            s_l = _dot(kw_ref[...], qh) + bias_ref[half]
            sc_ref[half] = s_c
            sl_ref[half] = s_l
            m_ref[half] = jnp.maximum(jnp.max(s_c, axis=0, keepdims=True), jnp.max(s_l, axis=0, keepdims=True))

    def consume(sl_ref, sc_ref, m_ref):
        outs = []
        for half in range(2):
            m = m_ref[half]
            p_c = jnp.exp2(sc_ref[half] - m)
            p_l = jnp.exp2(sl_ref[half] - m)
            l = jnp.sum(p_c, axis=0, keepdims=True) + jnp.sum(p_l, axis=0, keepdims=True)
            o_t = _dot(vct_ref[...], p_c.astype(MM)) + _dot(vwt_ref[...], p_l.astype(MM))
            outs.append(o_t * (1.0 / l))
        top = lax.broadcasted_iota(jnp.int32, outs[0].shape, 0) < HEAD
        o_ref[...] = jnp.where(top, outs[0], outs[1]).T.astype(o_ref.dtype)

    @pl.when(b == 0)
    def _():
        scores(*bufs[0])

    for par in range(2):
        @pl.when((b > 0) & (b < nb) & (b % 2 == par))
        def _(par=par):
            scores(*bufs[par])
            consume(*bufs[1 - par])

    @pl.when(b == nb)
    def _():
        consume(*bufs[(nb - 1) % 2])


def _na_geometry(rows):
    kr = min(NA_KR, rows)
    assert kr == NA_KR and rows % NA_RB == 0 and rows >= NA_WIN and (rows - NA_WIN) % 2 == 0
    nb = rows // NA_RB
    ws = np.clip(NA_RB * np.arange(nb) - kr // 2, 0, rows - NA_WIN)

    def rowpart(b):
        r = NA_RB * b + np.arange(NA_RB)
        r0 = np.clip(r - kr // 2, 0, rows - kr)
        key = ws[b] + np.arange(NA_WIN)
        valid = (key[:, None] >= r0[None, :]) & (key[:, None] < r0[None, :] + kr)
        brow = np.clip(key[:, None] - r[None, :] + (NA_KR - 1), 0, 2 * NA_KR - 2)
        return valid, brow

    variant_blocks = [0, min(1, nb - 1), nb - 1]
    variant_of = np.where(np.arange(nb) == 0, 0, np.where(np.arange(nb) == nb - 1, 2, 1))
    for b in range(nb):
        vb, bb = rowpart(b)
        vv, bv = rowpart(variant_blocks[variant_of[b]])
        assert (vb == vv).all() and (np.where(vb, bb, 0) == np.where(vv, bv, 0)).all()
    return nb, ws, [rowpart(b) for b in variant_blocks]


def _na_bias_table(rpb, rows):
    _, _, parts = _na_geometry(rows)
    kw = min(NA_KC, GRID_W)
    w = np.arange(GRID_W)
    c0 = np.clip(w - kw // 2, 0, GRID_W - kw)
    kc = np.arange(GRID_W)
    cvalid = (kc[:, None] >= c0[None, :]) & (kc[:, None] < c0[None, :] + kw)
    bcol = np.clip(kc[:, None] - w[None, :] + (NA_KC - 1), 0, 2 * NA_KC - 2)
    n_bias = 2 * NA_KC - 1
    nq = NA_RB * GRID_W
    x = jnp.arange(NA_RB * n_bias)[:, None, None]
    qi = np.arange(nq)
    sel = ((x // n_bias == jnp.asarray(qi // GRID_W)[None, None, :])
           & (x % n_bias == jnp.asarray(bcol[:, qi % GRID_W])[None, :, :])).astype(F32)
    rows_sel = jnp.stack([rpb[:, brow] for _, brow in parts])
    rows_sel = rows_sel.reshape(rows_sel.shape[:3] + (NA_RB * n_bias,))
    tab = jnp.einsum('vhkx,xcq->vhkcq', rows_sel, sel, precision=lax.Precision.HIGHEST)
    valid = np.stack([(rvalid[:, None, :, None] & cvalid[None, :, None, :]).reshape(NA_WIN, GRID_W, nq)
                      for rvalid, _ in parts])
    tab = jnp.where(valid[:, None], tab * LOG2E, NEG)
    return tab.reshape(len(parts), rpb.shape[0], NA_WIN * GRID_W, nq).astype(F32)


def _na_attention(k, qvt, rpb, *, n_lat):
    t, width = k.shape
    nc = t - n_lat
    nh = width // LANES
    rows = n_lat // GRID_W
    ctx_blk = n_lat // nc
    nb, ws_np, _ = _na_geometry(rows)
    table = _na_bias_table(rpb, rows)
    nq, win = NA_RB * GRID_W, NA_WIN * GRID_W
    assert nc % nq == 0

    def ws(b):
        return jnp.clip(NA_RB // 2 * b - NA_KR // 4, 0, (rows - NA_WIN) // 2) * (2 * GRID_W)

    def variant(b):
        return jnp.where(b == 0, 0, jnp.where(b == nb - 1, 2, 1))

    assert (ws_np == np.clip(NA_RB * np.arange(nb) - NA_KR // 2, 0, rows - NA_WIN)).all()
    cur = lambda b: jnp.minimum(b, nb - 1)
    prev = lambda b: jnp.maximum(b - 1, 0)
    o_lat = pl.pallas_call(
        functools.partial(_na_kernel, local=True, nb=nb),
        grid=(nh, nb + 1),
        in_specs=[pl.BlockSpec((LANES, nq), lambda p, b: (p, cur(b))),
                  pl.BlockSpec((pl.Element(win), pl.Element(LANES)), lambda p, b: (ws(cur(b)), p * LANES)),
                  pl.BlockSpec((pl.Element(LANES), pl.Element(win)), lambda p, b: ((nh + p) * LANES, ws(prev(b)))),
                  pl.BlockSpec((nc, LANES), lambda p, b: (ctx_blk, p)),
                  pl.BlockSpec((LANES, nc), lambda p, b: (nh + p, ctx_blk)),
                  pl.BlockSpec((None, 2, win, nq), lambda p, b: (variant(cur(b)), p, 0, 0))],
        out_specs=pl.BlockSpec((nq, LANES), lambda p, b: (prev(b), p)),
        out_shape=jax.ShapeDtypeStruct((n_lat, width), MM),
        scratch_shapes=[pltpu.VMEM((2, win, nq), F32)] * 2 + [pltpu.VMEM((2, nc, nq), F32)] * 2
                       + [pltpu.VMEM((2, 1, nq), F32)] * 2,
        compiler_params=_cparams("parallel", "arbitrary"),
        name="na_attn",
    )(qvt, k, qvt, k, qvt, table)
    o_ctx = pl.pallas_call(
        functools.partial(_na_kernel, local=False),
        grid=(nh,),
        in_specs=[pl.BlockSpec((LANES, nc), lambda p: (p, ctx_blk)),
                  pl.BlockSpec((nc, LANES), lambda p: (ctx_blk, p)),
                  pl.BlockSpec((LANES, nc), lambda p: (nh + p, ctx_blk))],
        out_specs=pl.BlockSpec((nc, LANES), lambda p: (0, p)),
        out_shape=jax.ShapeDtypeStruct((nc, width), MM),
        compiler_params=_cparams("parallel"),
        name="na_attn_ctx",
    )(qvt, k, qvt)
    return o_lat, o_ctx


def _rope_tables(n_lat, n_ctx):
    tok = jnp.arange(n_lat)
    row = (tok // GRID_W).astype(F32)
    col = (tok % GRID_W).astype(F32)
    n_freq = HEAD // 4
    inv = ROPE_BASE ** (-jnp.arange(n_freq, dtype=F32) / n_freq)
    ang = jnp.concatenate([row[:, None] * inv, col[:, None] * inv], axis=-1)
    cos, sin = jnp.cos(ang), jnp.sin(ang)
    cos = jnp.concatenate([jnp.tile(cos, (1, 4)), jnp.ones((n_ctx, LANES), F32)], axis=0)
    sin = jnp.concatenate([jnp.concatenate([-sin, sin, -sin, sin], axis=1), jnp.zeros((n_ctx, LANES), F32)], axis=0)
    return cos, sin


def _pick_tn(f, cap):
    return max(m for m in range(LANES, cap + 1, LANES) if f % m == 0)


def _pick_tm(t, cap):
    return max(m for m in range(8, cap + 1, 8) if t % m == 0)


def kernel(x, c, ctx, c_ctx, ada_w, ada_b, norm_pre, norm_post, ssd_w_in, ssd_conv_w, ssd_conv_b, ssd_dt_bias, ssd_a_log, ssd_d, ssd_norm, ssd_w_out, dif_w_in, dif_lam_q1, dif_lam_k1, dif_lam_q2, dif_lam_k2, dif_subln, dif_w_out, na_w_in, na_rpb, na_w_out):
    n_lat, d = x.shape[1], x.shape[2]
    n_ctx = ctx.shape[1]
    depth = ada_w.shape[0]
    xa = jnp.concatenate([x[0], ctx[0]], axis=0)
    t = xa.shape[0]
    mods = _modulation(c, c_ctx, ada_w, ada_b)
    rope = None
    tm_in = _pick_tm(t, 1280)
    tm_sum = _pick_tm(t, 640)
    tm_lat = _pick_tm(n_lat, 512)
    q_scale = HEAD ** -0.5 * LOG2E
    ia = ib = ic = 0
    for i in range(depth):
        last = i == depth - 1
        out_rows = n_lat if last else t
        shift, scale, gate = (mods[i, 0:2, k * d:(k + 1) * d] for k in range(3))
        nw = norm_pre[i][None, :]
        pw = norm_post[i][None, :]
        inproj = functools.partial(_inproj, xa, nw, scale, shift, n_lat=n_lat, tm=tm_in)
        kind = i % 3
        if kind == 0:
            w_in = ssd_w_in[ia]
            inner = ssd_w_out.shape[1]
            gn2 = w_in.shape[1] - 2 * inner - 2 * SSD_HEADS
            n_main = 2 * inner + gn2
            w_dt = jnp.zeros((d, 2 * LANES), F32)
            w_dt = w_dt.at[:, :SSD_HEADS].set(w_in[:, n_main:n_main + SSD_HEADS])
            w_dt = w_dt.at[:, LANES:LANES + SSD_HEADS].set(w_in[:, n_main + SSD_HEADS:])
            w_all = jnp.concatenate([w_in[:, :n_main], w_dt], axis=1).astype(MM)
            u = inproj(w_all, out_dtype=F32, tn=_pick_tn(w_all.shape[1], 1280), name="ssd_inproj")
            cw, cb = ssd_conv_w[ia], ssd_conv_b[ia][None, :]
            xs = _conv_silu(u, inner, inner, cw[:, :inner], cb[:, :inner], n_lat=n_lat, tm=n_ctx, tc=1024,
                            out_dtype=F32, name="ssd_conv_x")
            bc = _conv_silu(u, 2 * inner, gn2, cw[:, inner:], cb[:, inner:], n_lat=n_lat, tm=n_ctx, tc=1024,
                            out_dtype=MM, name="ssd_conv_bc")
            y = _ssd_scan(xs, bc, u, n_main // LANES, ssd_dt_bias[ia], ssd_a_log[ia], ssd_d[ia], n_lat=n_lat)
            tm = tm_lat if last else tm_sum
            y_ops = [(y, pl.BlockSpec((None, tm, inner), lambda r, dd=dd: (dd, r, 0))) for dd in range(2)]
            z_op = (u, pl.BlockSpec((tm, inner), lambda r: (r, 0)))
            xa = _outproj(y_ops, "sum", z_op, ssd_norm[ia][None, :], ssd_w_out[ia].astype(MM), xa, pw, gate,
                          n_lat=n_lat, tm=tm, out_rows=out_rows, name="ssd_outproj")
            ia += 1
        else:
            if kind == 1:
                w_in, w_out, width = dif_w_in[ib], dif_w_out[ib], dif_w_out.shape[1]
                if rope is None:
                    rope = _rope_tables(n_lat, n_ctx)
                rope_k, rope_q, n_rope = rope, tuple(tab.T for tab in rope), width // TN_QKV
            else:
                w_in, w_out, width = na_w_in[ic], na_w_out[ic], d
                rope_k = rope_q = None
                n_rope = 0
            k = inproj(w_in[:, width:2 * width].astype(MM), out_dtype=MM, tn=TN_QKV, n_rope=n_rope, rope=rope_k,
                       name="attn_inproj_k")
            w_qv = jnp.concatenate([w_in[:, :width], w_in[:, 2 * width:3 * width]], axis=1).T.astype(MM)
            qvt = inproj(w_qv, out_dtype=MM, tn=TN_QKV, n_q=width // TN_QKV, n_rope=n_rope, q_scale=q_scale,
                         rope=rope_q, transposed=True, name="attn_inproj_qvt")
            g = inproj(w_in[:, 3 * width:].astype(MM), out_dtype=F32, tn=TN_WIDE, name="attn_inproj_g")
            if kind == 1:
                lam_init = 0.8 - 0.6 * math.exp(-0.3 * i)
                lam4 = jnp.stack([dif_lam_q1[ib], dif_lam_k1[ib], dif_lam_q2[ib], dif_lam_k2[ib]])
                o_lat, o_ctx = _diff_attention(k, qvt, lam4, dif_subln[ib][None, :], n_lat=n_lat,
                                               lam_init=lam_init, tq=min(1024, n_lat), tk=min(2048, n_lat))
                ib += 1
            else:
                o_lat, o_ctx = _na_attention(k, qvt, na_rpb[ic], n_lat=n_lat)
                ic += 1
            g_op = (g, pl.BlockSpec((n_ctx, width), lambda r: (r, 0)))
            xa = _outproj(_attn_out_ops(o_lat, o_ctx, n_ctx), "select", g_op, None, w_out.astype(MM), xa, pw,
                          gate, n_lat=n_lat, tm=n_ctx, out_rows=out_rows, name="attn_outproj")
    return xa[:n_lat][None]
```
